```python
import math, functools
import jax
import jax.numpy as jnp
from jax import lax
import numpy as np

D_MODEL = 1024
BATCH = 4
SEQ = 8192
DEPTH = 4

CTX_LEN = 256
GRID_W = 64
N_MIXERS = 4
GROUP_W = D_MODEL // N_MIXERS
D_MIX = N_MIXERS * GROUP_W
HEAD_DIM = 64
N_HEADS = GROUP_W // HEAD_DIM
CHUNK = 64
CONV_W = 3
EPS = 1e-6
D_FF = 2816
N_MOD = 9
GLA_RANK = 16
GLA_NORMALISER = 16.0
SSD_STATE = 64
SSD_GROUPS = 2
SSD_CONV_CH = GROUP_W + 2 * SSD_GROUPS * SSD_STATE
S5_CH = 16
S5_GROUPS = GROUP_W // S5_CH
S5_STATE = 64
GLA_COLS = 4 * GROUP_W + 2 * GLA_RANK
SSD_COLS = SSD_CONV_CH + GROUP_W + 2 * N_HEADS
MLSTM_COLS = 4 * GROUP_W + 4 * N_HEADS
S5_COLS = GROUP_W
IN_COLS = GLA_COLS + SSD_COLS + MLSTM_COLS + S5_COLS

kernel_name = 'hybrid_parallel_mixer_diffusion_trunk'


def _rmsnorm(t, g):
    tf = t.astype(jnp.float32)
    return (tf * lax.rsqrt(jnp.mean(tf * tf, axis=-1, keepdims=True) + EPS)).astype(t.dtype) * g


def _modnorm(t, m, j, g):
    return _rmsnorm(t, g) * (1.0 + m[:, :, 3 * j + 1]) + m[:, :, 3 * j]


def _gated_add(t, m, j, y, weight):
    return t + (weight * m[:, :, 3 * j + 2] * y).astype(t.dtype)


def _swiglu(h, w_in, w_out):
    gate, up = jnp.split(h @ w_in, 2, axis=-1)
    return (jax.nn.silu(gate) * up) @ w_out


def _split(p, sizes):
    idx = [int(i) for i in np.cumsum(sizes)[:-1]]
    return jnp.split(p, idx, axis=-1)


def _heads(t, n=N_HEADS):
    b, l, _ = t.shape
    return t.reshape(b, l, n, -1).transpose(0, 2, 1, 3)


def _merge(t):
    b, h, l, d = t.shape
    return t.transpose(0, 2, 1, 3).reshape(b, l, h * d)


def _head_rmsnorm(o, g):
    o = o * lax.rsqrt(jnp.mean(o * o, axis=-1, keepdims=True) + EPS)
    return _merge(o) * g


def _dwconv(t, w, bias):
    y = lax.conv_general_dilated(t, w.astype(t.dtype)[:, None, :], window_strides=(1,), padding='SAME',
                                 dimension_numbers=('NWC', 'WIO', 'NWC'), feature_group_count=t.shape[-1])
    return y + bias


def _to_col_major(t, rows):
    b, l, ch = t.shape
    return t.reshape(b, rows, GRID_W, ch).transpose(0, 2, 1, 3).reshape(b, l, ch)


def _to_row_major(t, rows):
    b, l, ch = t.shape
    return t.reshape(b, GRID_W, rows, ch).transpose(0, 2, 1, 3).reshape(b, l, ch)


def _to_chunks(t):
    b, h, l = t.shape[:3]
    return jnp.moveaxis(t.reshape((b, h, l // CHUNK, CHUNK) + t.shape[3:]), 2, 0)


def _from_chunks(t):
    t = jnp.moveaxis(t, 0, 2)
    return t.reshape(t.shape[:2] + (-1,) + t.shape[4:])


def _causal_mask():
    return jnp.tril(jnp.ones((CHUNK, CHUNK), dtype=bool))


def _flip_time(t, reverse):
    return jnp.flip(t, axis=2) if reverse else t


def _bidirectional(runs, ctx_ins, lat_ins):
    out_c, out_l = [], []
    for d in range(2):
        rev = d == 1
        oc, state = runs[d](tuple(_flip_time(t, rev) for t in ctx_ins[d]), None)
        ol, _ = runs[d](tuple(_flip_time(t, rev) for t in lat_ins[d]), state)
        out_c.append(_flip_time(oc, rev))
        out_l.append(_flip_time(ol, rev))
    return out_c[0] + out_c[1], out_l[0] + out_l[1]


def _gla_run(inp, state):
    q, k, v, g = inp
    if state is None:
        state = jnp.zeros(q.shape[:2] + (q.shape[-1], v.shape[-1]), jnp.float32)
    mask = _causal_mask()

    def step(s, blk):
        qc, kc, vc, gc = blk
        b = jnp.cumsum(gc, axis=2)
        q_t = qc * jnp.exp(b)
        att = jnp.where(mask, jnp.einsum('bhtd,bhsd->bhts', q_t, kc * jnp.exp(-b)), 0.0)
        o = jnp.einsum('bhts,bhsv->bhtv', att, vc) + jnp.einsum('bhtd,bhdv->bhtv', q_t, s)
        b_last = b[:, :, -1:, :]
        s = jnp.exp(b_last[:, :, 0])[..., None] * s + jnp.einsum('bhsd,bhsv->bhdv', kc * jnp.exp(b_last - b), vc)
        return s, o

    s, o = lax.scan(step, state, tuple(_to_chunks(t) for t in inp))
    return _from_chunks(o), s


def _gla_mixer(pc, px, w_lr2, b_lr2, g_out):
    def prep(p):
        q, k, v, r, lr_f, lr_b = _split(p, [GROUP_W, GROUP_W, GROUP_W, GROUP_W, GLA_RANK, GLA_RANK])
        q = _heads(q) * HEAD_DIM ** -0.5
        k, v = _heads(k), _heads(v)
        ins = tuple((q, k, v, _heads(jax.nn.log_sigmoid(lr @ w_lr2[d] + b_lr2[d]) / GLA_NORMALISER))
                    for d, lr in enumerate((lr_f, lr_b)))
        return ins, r

    ins_c, r_c = prep(pc)
    ins_x, r_x = prep(px)
    o_c, o_x = _bidirectional((_gla_run, _gla_run), ins_c, ins_x)
    return (_head_rmsnorm(o_c, g_out) * jax.nn.silu(r_c), _head_rmsnorm(o_x, g_out) * jax.nn.silu(r_x))


def _ssd_run(inp, state):
    c_in, b_in, xdt, a = inp
    if state is None:
        state = jnp.zeros(b_in.shape[:2] + (b_in.shape[-1], xdt.shape[-1]), jnp.float32)
    mask = _causal_mask()

    def step(s, blk):
        cc, bc, xc, ac = blk
        cum = jnp.cumsum(ac, axis=-1)
        decay = jnp.exp(jnp.where(mask, cum[..., :, None] - cum[..., None, :], -jnp.inf))
        scores = jnp.einsum('bhtn,bhsn->bhts', cc, bc) * decay
        y = (jnp.einsum('bhts,bhsp->bhtp', scores, xc)
             + jnp.exp(cum)[..., None] * jnp.einsum('bhtn,bhnp->bhtp', cc, s))
        w = jnp.exp(cum[..., -1:] - cum)[..., None]
        s = jnp.exp(cum[..., -1])[..., None, None] * s + jnp.einsum('bhsn,bhsp->bhnp', bc * w, xc)
        return s, y

    s, y = lax.scan(step, state, tuple(_to_chunks(t) for t in inp))
    return _from_chunks(y), s


def _ssd_mixer(pc, px, conv_w, conv_b, a_log, dt_bias, d_skip, g_out):
    hpg = N_HEADS // SSD_GROUPS
    neg_a = -jnp.exp(a_log.astype(jnp.float32))

    def prep(p):
        b, l, _ = p.shape
        xbc, z, dt = _split(p, [SSD_CONV_CH, GROUP_W, 2 * N_HEADS])
        xs, bm, cm = _split(jax.nn.silu(_dwconv(xbc, conv_w, conv_b)),
                            [GROUP_W, SSD_GROUPS * SSD_STATE, SSD_GROUPS * SSD_STATE])
        xs = _heads(xs)
        bm = jnp.repeat(_heads(bm, SSD_GROUPS), hpg, axis=1)
        cm = jnp.repeat(_heads(cm, SSD_GROUPS), hpg, axis=1)
        dt = jax.nn.softplus(dt.reshape(b, l, 2, N_HEADS) + dt_bias).transpose(2, 0, 3, 1)
        ins = tuple((cm, bm, xs * dt[d][..., None], dt[d] * neg_a[d][:, None]) for d in range(2))
        return ins, xs, z

    ins_c, xs_c, z_c = prep(pc)
    ins_x, xs_x, z_x = prep(px)
    o_c, o_x = _bidirectional((_ssd_run, _ssd_run), ins_c, ins_x)

    def out(o, xs, z):
        return _rmsnorm(_merge(o + d_skip[:, None, None] * xs) * jax.nn.silu(z), g_out)

    return out(o_c, xs_c, z_c), out(o_x, xs_x, z_x)


def _mlstm_run(inp, state):
    q, k, v, ig, lf = inp
    if state is None:
        bh = q.shape[:2]
        state = (jnp.zeros(bh + (q.shape[-1], v.shape[-1]), jnp.float32),
                 jnp.zeros(bh + (q.shape[-1],), jnp.float32),
                 jnp.zeros(bh, jnp.float32))
    mask = _causal_mask()

    def step(carry, blk):
        c_s, n_s, m_s = carry
        qc, kc, vc, ic, fc = blk
        f_cum = jnp.cumsum(fc, axis=-1)
        log_intra = jnp.where(mask, f_cum[..., :, None] - f_cum[..., None, :] + ic[..., None, :], -jnp.inf)
        log_inter = f_cum + m_s[..., None]
        m = jnp.maximum(log_inter, jnp.max(log_intra, axis=-1))
        w = jnp.einsum('bhtd,bhsd->bhts', qc, kc) * jnp.exp(log_intra - m[..., None])
        w_inter = jnp.exp(log_inter - m)
        num = jnp.einsum('bhts,bhsv->bhtv', w, vc) + w_inter[..., None] * jnp.einsum('bhtd,bhdv->bhtv', qc, c_s)
        den = jnp.sum(w, axis=-1) + w_inter * jnp.einsum('bhtd,bhd->bht', qc, n_s)
        h = num / jnp.maximum(jnp.abs(den), jnp.exp(-m))[..., None]
        m_new = m[..., -1]
        decay = jnp.exp(f_cum[..., -1] + m_s - m_new)
        k_w = kc * jnp.exp(f_cum[..., -1:] - f_cum + ic - m_new[..., None])[..., None]
        c_s = decay[..., None, None] * c_s + jnp.einsum('bhsd,bhsv->bhdv', k_w, vc)
        n_s = decay[..., None] * n_s + jnp.sum(k_w, axis=2)
        return (c_s, n_s, m_new), h

    final, h = lax.scan(step, state, tuple(_to_chunks(t) for t in inp))
    return _from_chunks(h), final


def _mlstm_mixer(pc, px, conv_w, conv_b, gate_bias, g_out):
    def prep(p):
        b, l, _ = p.shape
        qk, v, o, gates = _split(p, [2 * GROUP_W, GROUP_W, GROUP_W, 4 * N_HEADS])
        q, k = jnp.split(jax.nn.silu(_dwconv(qk, conv_w, conv_b)), 2, axis=-1)
        q, k, v = _heads(q), _heads(k) * HEAD_DIM ** -0.5, _heads(v)
        g = (gates.reshape(b, l, 2, 2, N_HEADS) + gate_bias).transpose(2, 3, 0, 4, 1)
        ins = tuple((q, k, v, g[d, 0], jax.nn.log_sigmoid(g[d, 1])) for d in range(2))
        return ins, o

    ins_c, o_c = prep(pc)
    ins_x, o_x = prep(px)
    h_c, h_x = _bidirectional((_mlstm_run, _mlstm_run), ins_c, ins_x)
    return (_head_rmsnorm(h_c, g_out) * jax.nn.sigmoid(o_c), _head_rmsnorm(h_x, g_out) * jax.nn.sigmoid(o_x))


def _s5_discretise(a_re, a_im, log_step, b_re, b_im):
    a_re = jnp.minimum(a_re.astype(jnp.float32), -1e-4)
    a_im = a_im.astype(jnp.float32)
    b_re, b_im = b_re.astype(jnp.float32), b_im.astype(jnp.float32)
    step = jnp.exp(log_step.astype(jnp.float32))[:, None]
    mag = jnp.exp(a_re * step)
    abar_re, abar_im = mag * jnp.cos(a_im * step), mag * jnp.sin(a_im * step)
    den = a_re * a_re + a_im * a_im
    nr, ni = abar_re - 1.0, abar_im
    coef_re = (nr * a_re + ni * a_im) / den
    coef_im = (ni * a_re - nr * a_im) / den
    bbar_re = coef_re[..., None] * b_re - coef_im[..., None] * b_im
    bbar_im = coef_re[..., None] * b_im + coef_im[..., None] * b_re
    return abar_re, abar_im, bbar_re, bbar_im


def _complex_linear_combine(e1, e2):
    ar1, ai1, br1, bi1 = e1
    ar2, ai2, br2, bi2 = e2
    return (ar2 * ar1 - ai2 * ai1, ar2 * ai1 + ai2 * ar1,
            ar2 * br1 - ai2 * bi1 + br2, ar2 * bi1 + ai2 * br1 + bi2)


def _s5_run(abar_re, abar_im, bbar_re, bbar_im, c_re, c_im, inp, state):
    (u,) = inp
    bu_re = jnp.einsum('bglj,gnj->bgln', u, bbar_re)
    bu_im = jnp.einsum('bglj,gnj->bgln', u, bbar_im)
    if state is not None:
        h_re0, h_im0 = state
        bu_re = bu_re.at[:, :, 0].add(abar_re * h_re0 - abar_im * h_im0)
        bu_im = bu_im.at[:, :, 0].add(abar_re * h_im0 + abar_im * h_re0)
    shape = (1, abar_re.shape[0], u.shape[2], abar_re.shape[1])
    a_re = jnp.broadcast_to(abar_re[None, :, None, :], shape)
    a_im = jnp.broadcast_to(abar_im[None, :, None, :], shape)
    _, _, h_re, h_im = lax.associative_scan(_complex_linear_combine, (a_re, a_im, bu_re, bu_im), axis=2)
    y = jnp.einsum('bgln,gjn->bglj', h_re, c_re) - jnp.einsum('bgln,gjn->bglj', h_im, c_im)
    return y, (h_re[:, :, -1], h_im[:, :, -1])


def _s5_mixer(pc, px, a_re, a_im, log_step, b_re, b_im, c_re, c_im, d_skip, w_glu, b_glu):
    c_re, c_im = c_re.astype(jnp.float32), c_im.astype(jnp.float32)
    runs = tuple(functools.partial(_s5_run, *_s5_discretise(a_re[d], a_im[d], log_step[d], b_re, b_im), c_re, c_im)
                 for d in range(2))

    def groups(p):
        b, l, _ = p.shape
        return (p.reshape(b, l, S5_GROUPS, S5_CH).transpose(0, 2, 1, 3),)

    u_c, u_x = groups(pc), groups(px)
    y_c, y_x = _bidirectional(runs, (u_c, u_c), (u_x, u_x))

    def out(y, p):
        b, _, l, _ = y.shape
        y = y.transpose(0, 2, 1, 3).reshape(b, l, GROUP_W) + d_skip * p
        g = jax.nn.gelu(y)
        return g * jax.nn.sigmoid(g @ w_glu + b_glu)

    return out(y_c, pc), out(y_x, px)


def _token_mixing(hc, hx, rows, w_in, gla_w_lr2, gla_b_lr2, gla_g_norm, ssd_conv_w, ssd_conv_b, ssd_a_log,
                  ssd_dt_bias, ssd_d, ssd_g_norm, ml_conv_w, ml_conv_b, ml_gate_bias, ml_g_norm, s5_a_re, s5_a_im,
                  s5_log_step, s5_b_re, s5_b_im, s5_c_re, s5_c_im, s5_d, s5_w_glu, s5_b_glu):
    sizes = [GLA_COLS, SSD_COLS, MLSTM_COLS, S5_COLS]
    gla_c, ssd_c, ml_c, s5_c = _split((hc @ w_in).astype(jnp.float32), sizes)
    gla_x, ssd_x, ml_x, s5_x = _split((hx @ w_in).astype(jnp.float32), sizes)
    a_c, a_x = _gla_mixer(gla_c, gla_x, gla_w_lr2, gla_b_lr2, gla_g_norm)
    b_c, b_x = _ssd_mixer(ssd_c, _to_col_major(ssd_x, rows), ssd_conv_w, ssd_conv_b, ssd_a_log, ssd_dt_bias,
                          ssd_d, ssd_g_norm)
    m_c, m_x = _mlstm_mixer(ml_c, ml_x, ml_conv_w, ml_conv_b, ml_gate_bias, ml_g_norm)
    d_c, d_x = _s5_mixer(s5_c, _to_col_major(s5_x, rows), s5_a_re, s5_a_im, s5_log_step, s5_b_re, s5_b_im,
                         s5_c_re, s5_c_im, s5_d, s5_w_glu, s5_b_glu)
    out_c = jnp.concatenate([a_c, b_c, m_c, d_c], axis=-1)
    out_x = jnp.concatenate([a_x, _to_row_major(b_x, rows), m_x, _to_row_major(d_x, rows)], axis=-1)
    return out_c, out_x


def setup_inputs(seed: int = 0) -> dict:
    key = jax.random.key(seed)
    ks = iter(jax.random.split(key, 48))
    f32 = jnp.float32
    L, D = DEPTH, D_MODEL

    def nrm(shape, scale):
        return jax.random.normal(next(ks), shape, f32) * scale

    def gain(shape):
        return 1.0 + nrm(shape, 0.02)

    def log_uniform(shape, lo, hi):
        return jax.random.uniform(next(ks), shape, f32, math.log(lo), math.log(hi))

    ssd_dt = jnp.exp(log_uniform((L, 2, N_HEADS), 1e-3, 1e-1))
    ml_gate_bias = jnp.stack([nrm((L, 2, N_HEADS), 0.1),
                              jnp.linspace(3.0, 6.0, N_HEADS, dtype=f32) + nrm((L, 2, N_HEADS), 0.1)], axis=2)
    return {
        'x': nrm((BATCH, SEQ, D), 1.0),
        'c': nrm((BATCH, D), 1.0),
        'ctx': nrm((BATCH, CTX_LEN, D), 1.0),
        'c_ctx': nrm((D,), 1.0),
        'w_mod': nrm((L, D, N_MOD * D), 0.5 * D ** -0.5),
        'b_mod': nrm((L, N_MOD * D), 0.02),
        'g_norm': gain((L, 3, D)),
        'ffn_w_in': nrm((L, 2, D, 2 * D_FF), D ** -0.5),
        'ffn_w_out': nrm((L, 2, D_FF, D), D_FF ** -0.5),
        'w_in': nrm((L, D, IN_COLS), D ** -0.5),
        'w_out': nrm((L, D_MIX, D), D_MIX ** -0.5),
        'gla_w_lr2': nrm((L, 2, GLA_RANK, GROUP_W), GLA_RANK ** -0.5),
        'gla_b_lr2': 1.0 + nrm((L, 2, GROUP_W), 0.1),
        'gla_g_norm': gain((L, GROUP_W)),
        'ssd_conv_w': nrm((L, CONV_W, SSD_CONV_CH), CONV_W ** -0.5),
        'ssd_conv_b': nrm((L, SSD_CONV_CH), 0.02),
        'ssd_a_log': jnp.log(jax.random.uniform(next(ks), (L, 2, N_HEADS), f32, 1.0, 16.0)),
        'ssd_dt_bias': ssd_dt + jnp.log(-jnp.expm1(-ssd_dt)),
        'ssd_d': gain((L, N_HEADS)),
        'ssd_g_norm': gain((L, GROUP_W)),
        'ml_conv_w': nrm((L, CONV_W, 2 * GROUP_W), CONV_W ** -0.5),
        'ml_conv_b': nrm((L, 2 * GROUP_W), 0.02),
        'ml_gate_bias': ml_gate_bias,
        'ml_g_norm': gain((L, GROUP_W)),
        's5_a_re': -0.5 + nrm((L, 2, S5_GROUPS, S5_STATE), 0.01),
        's5_a_im': jnp.pi * jnp.arange(S5_STATE, dtype=f32) + nrm((L, 2, S5_GROUPS, S5_STATE), 0.01),
        's5_log_step': log_uniform((L, 2, S5_GROUPS), 1e-3, 1e-1),
        's5_b_re': nrm((L, S5_GROUPS, S5_STATE, S5_CH), (2 * S5_CH) ** -0.5),
        's5_b_im': nrm((L, S5_GROUPS, S5_STATE, S5_CH), (2 * S5_CH) ** -0.5),
        's5_c_re': nrm((L, S5_GROUPS, S5_CH, S5_STATE), (2 * S5_STATE) ** -0.5),
        's5_c_im': nrm((L, S5_GROUPS, S5_CH, S5_STATE), (2 * S5_STATE) ** -0.5),
        's5_d': nrm((L, GROUP_W), 1.0),
        's5_w_glu': nrm((L, GROUP_W, GROUP_W), GROUP_W ** -0.5),
        's5_b_glu': nrm((L, GROUP_W), 0.02),
        'g_final': gain((D,)),
    }


def reference(x, c, ctx, c_ctx, w_mod, b_mod, g_norm, ffn_w_in, ffn_w_out, w_in, w_out, gla_w_lr2, gla_b_lr2,
              gla_g_norm, ssd_conv_w, ssd_conv_b, ssd_a_log, ssd_dt_bias, ssd_d, ssd_g_norm, ml_conv_w, ml_conv_b,
              ml_gate_bias, ml_g_norm, s5_a_re, s5_a_im, s5_log_step, s5_b_re, s5_b_im, s5_c_re, s5_c_im, s5_d,
              s5_w_glu, s5_b_glu, g_final):
    bsz = x.shape[0]
    rows = x.shape[1] // GRID_W
    for l in range(DEPTH):
        last = l == DEPTH - 1
        m_x = (jax.nn.silu(c) @ w_mod[l] + b_mod[l]).reshape(bsz, 1, N_MOD, D_MODEL)
        m_c = (jax.nn.silu(c_ctx) @ w_mod[l] + b_mod[l]).reshape(1, 1, N_MOD, D_MODEL)
        x = _gated_add(x, m_x, 0, _swiglu(_modnorm(x, m_x, 0, g_norm[l, 0]), ffn_w_in[l, 0], ffn_w_out[l, 0]), 0.5)
        ctx = _gated_add(ctx, m_c, 0, _swiglu(_modnorm(ctx, m_c, 0, g_norm[l, 0]), ffn_w_in[l, 0], ffn_w_out[l, 0]), 0.5)
        o_c, o_x = _token_mixing(
            _modnorm(ctx, m_c, 1, g_norm[l, 1]), _modnorm(x, m_x, 1, g_norm[l, 1]), rows, w_in[l],
            gla_w_lr2[l], gla_b_lr2[l], gla_g_norm[l], ssd_conv_w[l], ssd_conv_b[l], ssd_a_log[l], ssd_dt_bias[l],
            ssd_d[l], ssd_g_norm[l], ml_conv_w[l], ml_conv_b[l], ml_gate_bias[l], ml_g_norm[l], s5_a_re[l],
            s5_a_im[l], s5_log_step[l], s5_b_re[l], s5_b_im[l], s5_c_re[l], s5_c_im[l], s5_d[l], s5_w_glu[l],
            s5_b_glu[l])
        x = _gated_add(x, m_x, 1, o_x @ w_out[l], 1.0)
        x = _gated_add(x, m_x, 2, _swiglu(_modnorm(x, m_x, 2, g_norm[l, 2]), ffn_w_in[l, 1], ffn_w_out[l, 1]), 0.5)
        if not last:
            ctx = _gated_add(ctx, m_c, 1, o_c @ w_out[l], 1.0)
            ctx = _gated_add(ctx, m_c, 2, _swiglu(_modnorm(ctx, m_c, 2, g_norm[l, 2]), ffn_w_in[l, 1], ffn_w_out[l, 1]), 0.5)
    return _rmsnorm(x, g_final)
```

```python
import functools

import jax
import jax.numpy as jnp
from jax import lax
from jax.experimental import pallas as pl
from jax.experimental.pallas import tpu as pltpu

F32 = jnp.float32
BF16 = jnp.bfloat16
HI = lax.Precision.HIGHEST

D_MODEL = 1024
D_FF = 2816
GRID_W = 64
GROUP_W = 256
HEAD_DIM = 64
N_HEADS = 4
CHUNK = 64
EPS = 1e-6
N_MOD = 9
GLA_RANK = 16
GLA_NORMALISER = 16.0
S5_CH = 16
S5_GROUPS = 16
S5_STATE = 64
S5_T = 16
S5_PAIRS = S5_GROUPS // 2
SMALL_W = 128
PROJ_W = 3072 + SMALL_W
MOD_ROWS = 8
VMEM_LIMIT = 56 * 1024 * 1024


def _params(sem, vmem=VMEM_LIMIT):
    return pltpu.CompilerParams(dimension_semantics=sem, vmem_limit_bytes=vmem)


def _mm(a, b):
    return jnp.dot(a.astype(BF16), b.astype(BF16), preferred_element_type=F32)


def _mm_nt(a, b):
    return lax.dot_general(a.astype(BF16), b.astype(BF16), (((1,), (1,)), ((), ())), preferred_element_type=F32)


def _mm_tn(a, b):
    return lax.dot_general(a.astype(BF16), b.astype(BF16), (((0,), (0,)), ((), ())), preferred_element_type=F32)


def _mm_hi(a, b):
    return jnp.dot(a, b, precision=HI, preferred_element_type=F32)


def _mm_nt_hi(a, b):
    return lax.dot_general(a, b, (((1,), (1,)), ((), ())), precision=HI, preferred_element_type=F32)


def _mm_tn_hi(a, b):
    return lax.dot_general(a, b, (((0,), (0,)), ((), ())), precision=HI, preferred_element_type=F32)


def _rms(t):
    return t * lax.rsqrt(jnp.mean(t * t, axis=-1, keepdims=True) + EPS)


def _modnorm(xv, g, shift, scale):
    return (_rms(xv) * g) * (1.0 + scale) + shift


def _const(shape):
    n = len(shape)
    return pl.BlockSpec(shape, lambda *_: (0,) * n, pipeline_mode=pl.Buffered(1))


def _mod_kernel(c_ref, w_ref, b_ref, o_ref):
    o_ref[...] = _mm_hi(jax.nn.silu(c_ref[...]), w_ref[...]) + b_ref[...]


def _modulation(cc, w_mod, b_mod):
    depth, d, n = w_mod.shape
    tn = 1024
    return pl.pallas_call(
        _mod_kernel, grid=(depth, n // tn),
        in_specs=[pl.BlockSpec((MOD_ROWS, d), lambda l, j: (0, 0)),
                  pl.BlockSpec((None, d, tn), lambda l, j: (l, 0, j)),
                  pl.BlockSpec((None, 1, tn), lambda l, j: (l, 0, j))],
        out_specs=pl.BlockSpec((None, MOD_ROWS, tn), lambda l, j: (l, 0, j)),
        out_shape=jax.ShapeDtypeStruct((depth, MOD_ROWS, n), F32),
        compiler_params=_params(("arbitrary", "arbitrary")), name="modulation",
    )(cc, w_mod, b_mod.reshape(depth, 1, n))


def _mod_spec(k, per_batch):
    if per_batch:
        return pl.BlockSpec((None, None, 1, D_MODEL), lambda b, i: (b, k, 0, 0))
    return pl.BlockSpec((None, None, 1, D_MODEL), lambda b, i: (0, k, 0, 0))


def _ffn_kernel(x_ref, sh_ref, sc_ref, gt_ref, g_ref, wi_ref, wo_ref, *rest, final):
    o_ref = rest[-1]
    xv = x_ref[...]
    h = _modnorm(xv, g_ref[...], sh_ref[...], sc_ref[...]).astype(BF16)
    gu = jnp.dot(h, wi_ref[...], preferred_element_type=F32)
    a = (jax.nn.silu(gu[:, :D_FF]) * gu[:, D_FF:]).astype(BF16)
    y = jnp.dot(a, wo_ref[...], preferred_element_type=F32)
    out = xv + (0.5 * gt_ref[...]) * y
    if final:
        out = _rms(out) * rest[0][...]
    o_ref[...] = out


def _ffn(x, modv, j, g, w_in, w_out, g_final=None):
    bsz, length, d = x.shape
    tm = min(512, length)
    per_batch = modv.shape[0] > 1
    final = g_final is not None
    ins = [x, modv, modv, modv, g, w_in, w_out]
    specs = [pl.BlockSpec((None, tm, d), lambda b, i: (b, i, 0)),
             _mod_spec(3 * j, per_batch), _mod_spec(3 * j + 1, per_batch), _mod_spec(3 * j + 2, per_batch),
             _const((1, d)), _const((d, 2 * D_FF)), _const((D_FF, d))]
    if final:
        ins.append(g_final)
        specs.append(_const((1, d)))
    return pl.pallas_call(
        functools.partial(_ffn_kernel, final=final), grid=(bsz, length // tm),
        in_specs=specs, out_specs=pl.BlockSpec((None, tm, d), lambda b, i: (b, i, 0)),
        out_shape=jax.ShapeDtypeStruct(x.shape, F32),
        compiler_params=_params(("arbitrary", "arbitrary")), name="ffn",
    )(*ins)


_PROJ_SPLIT = (("gla", 0, 1024, False), ("ml_qk", 1792, 2304, False), ("ml_vo", 2304, 2816, False),
               ("small", 3072, 3200, False), ("ssd_xbc", 1024, 1536, True), ("ssd_z", 1536, 1792, True),
               ("s5", 2816, 3072, True), ("small_cm", 3072, 3200, True))


def _inproj_kernel(x_ref, sh_ref, sc_ref, g_ref, w_ref, *outs, colmajor, rt):
    h = _modnorm(x_ref[...], g_ref[...], sh_ref[...], sc_ref[...]).astype(BF16)
    y = jnp.dot(h, w_ref[...], preferred_element_type=F32)
    for o_ref, (_, lo, hi, cm) in zip(outs, _PROJ_SPLIT):
        v = y[:, lo:hi]
        if cm and colmajor:
            v = jnp.swapaxes(v.reshape(rt, GRID_W, hi - lo), 0, 1)
        o_ref[...] = v


def _inproj(x, modv, g, w, colmajor):
    bsz, length, d = x.shape
    tm = min(512, length)
    rt = tm // GRID_W
    rows = length // GRID_W
    per_batch = modv.shape[0] > 1
    shapes, specs = [], []
    for _, lo, hi, cm in _PROJ_SPLIT:
        n = hi - lo
        if cm and colmajor:
            shapes.append(jax.ShapeDtypeStruct((bsz, GRID_W, rows, n), F32))
            specs.append(pl.BlockSpec((None, GRID_W, rt, n), lambda b, i: (b, 0, i, 0)))
        else:
            shapes.append(jax.ShapeDtypeStruct((bsz, length, n), F32))
            specs.append(pl.BlockSpec((None, tm, n), lambda b, i: (b, i, 0)))
    outs = pl.pallas_call(
        functools.partial(_inproj_kernel, colmajor=colmajor, rt=rt), grid=(bsz, length // tm),
        in_specs=[pl.BlockSpec((None, tm, d), lambda b, i: (b, i, 0)),
                  _mod_spec(3, per_batch), _mod_spec(4, per_batch), _const((1, d)), _const((d, PROJ_W))],
        out_specs=specs, out_shape=shapes,
        compiler_params=_params(("arbitrary", "arbitrary")), name="inproj",
    )(x, modv, modv, g, w)
    res = {}
    for o, (name, lo, hi, cm) in zip(outs, _PROJ_SPLIT):
        res[name] = o.reshape(bsz, length, hi - lo)
    return res


def _conv_kernel(x_ref, p_ref, n_ref, w_ref, b_ref, o_ref, *, tb):
    i = pl.program_id(1)
    nb = pl.num_programs(1)
    x = x_ref[...]
    rid = lax.broadcasted_iota(jnp.int32, x.shape, 0)
    prev = jnp.where(i > 0, p_ref[7:8, :], 0.0)
    nxt = jnp.where(i < nb - 1, n_ref[0:1, :], 0.0)
    xm = jnp.where(rid == 0, prev, pltpu.roll(x, 1, 0))
    xp = jnp.where(rid == tb - 1, nxt, pltpu.roll(x, tb - 1, 0))
    y = w_ref[0:1, :] * xm + w_ref[1:2, :] * x + w_ref[2:3, :] * xp + b_ref[...]
    o_ref[...] = jax.nn.silu(y)


def _conv_silu(x, w, b):
    bsz, length, ch = x.shape
    tb = min(1024, length)
    t8 = tb // 8
    last8 = length // 8 - 1
    return pl.pallas_call(
        functools.partial(_conv_kernel, tb=tb), grid=(bsz, length // tb),
        in_specs=[pl.BlockSpec((None, tb, ch), lambda bb, i: (bb, i, 0)),
                  pl.BlockSpec((None, 8, ch), lambda bb, i: (bb, jnp.maximum(i * t8 - 1, 0), 0)),
                  pl.BlockSpec((None, 8, ch), lambda bb, i: (bb, jnp.minimum((i + 1) * t8, last8), 0)),
                  _const((3, ch)), _const((1, ch))],
        out_specs=pl.BlockSpec((None, tb, ch), lambda bb, i: (bb, i, 0)),
        out_shape=jax.ShapeDtypeStruct(x.shape, F32),
        compiler_params=_params(("arbitrary", "arbitrary")), name="conv_silu",
    )(x, x, x, w, b)


CPB = 4
SCAN_TB = CPB * CHUNK


def _scan_block(d, i, nb):
    return jnp.where(d == 0, i, nb - 1 - i)


def _chunk_masks(d):
    row = lax.broadcasted_iota(jnp.int32, (CHUNK, CHUNK), 0)
    col = lax.broadcasted_iota(jnp.int32, (CHUNK, CHUNK), 1)
    mask = jnp.where(d == 0, row - col, col - row) >= 0
    return mask, mask.astype(F32)


def _chunk_rows(d, cc):
    c = jnp.where(d == 0, cc, CPB - 1 - cc)
    return pl.ds(pl.multiple_of(c * CHUNK, CHUNK), CHUNK)


def _scan_specs(nb):
    def blk(width, colblock):
        return pl.BlockSpec((None, SCAN_TB, width), lambda b, d, i: (b, _scan_block(d, i, nb), colblock))
    out = pl.BlockSpec((None, None, SCAN_TB, GROUP_W), lambda b, d, i: (d, b, _scan_block(d, i, nb), 0))
    state = pl.BlockSpec((None, None, N_HEADS, HEAD_DIM, HEAD_DIM), lambda b, d, i: (b, d, 0, 0, 0))
    return blk, out, state


def _dir_spec(shape):
    n = len(shape)
    return pl.BlockSpec((None,) + shape, lambda b, d, i: (d,) + (0,) * n)


def _gla_kernel(q_ref, k_ref, v_ref, sm_ref, w_ref, b_ref, s0_ref, o_ref, sf_ref, s_ref):
    d = pl.program_id(1)
    i = pl.program_id(2)

    @pl.when(i == 0)
    def _():
        s_ref[...] = s0_ref[...]

    mask, maskf = _chunk_masks(d)
    for cc in range(CPB):
        rows = _chunk_rows(d, cc)
        sm = sm_ref[rows, :]
        lr = jnp.where(d == 0, sm[:, 0:GLA_RANK], sm[:, GLA_RANK:2 * GLA_RANK])
        g = jax.nn.log_sigmoid(_mm_hi(lr, w_ref[...]) + b_ref[...]) * (1.0 / GLA_NORMALISER)
        bc = _mm_hi(maskf, g)
        bt = jnp.sum(g, axis=0, keepdims=True)
        k = k_ref[rows, :]
        v = v_ref[rows, :]
        qt = q_ref[rows, :] * (HEAD_DIM ** -0.5) * jnp.exp(bc)
        km = k * jnp.exp(-bc)
        ks = k * jnp.exp(bt - bc)
        ebt = jnp.exp(bt)
        outs = []
        for h in range(N_HEADS):
            sl = slice(h * HEAD_DIM, (h + 1) * HEAD_DIM)
            att = jnp.where(mask, _mm_nt(qt[:, sl], km[:, sl]), 0.0)
            st = s_ref[h]
            outs.append(_mm(att, v[:, sl]) + _mm_nt(qt[:, sl], st))
            s_ref[h] = st * ebt[:, sl] + _mm_tn(v[:, sl], ks[:, sl])
        o_ref[rows, :] = jnp.concatenate(outs, axis=-1)

    @pl.when(i == pl.num_programs(2) - 1)
    def _():
        sf_ref[...] = s_ref[...]


def _gla(proj, small, w_lr2, b_lr2, s0):
    bsz, length, _ = proj.shape
    nb = length // SCAN_TB
    blk, out, state = _scan_specs(nb)
    return pl.pallas_call(
        _gla_kernel, grid=(bsz, 2, nb),
        in_specs=[blk(GROUP_W, 0), blk(GROUP_W, 1), blk(GROUP_W, 2), blk(SMALL_W, 0),
                  _dir_spec((GLA_RANK, GROUP_W)), _dir_spec((1, GROUP_W)), state],
        out_specs=[out, state],
        out_shape=[jax.ShapeDtypeStruct((2, bsz, length, GROUP_W), F32), jax.ShapeDtypeStruct(s0.shape, F32)],
        scratch_shapes=[pltpu.VMEM((N_HEADS, HEAD_DIM, HEAD_DIM), F32)],
        compiler_params=_params(("arbitrary",) * 3), name="gla",
    )(proj, proj, proj, small, w_lr2, b_lr2, s0)


def _ssd_kernel(x_ref, b_ref, c_ref, sm_ref, al_r, al_c, db_r, db_c, s0_ref, o_ref, sf_ref, s_ref):
    d = pl.program_id(1)
    i = pl.program_id(2)

    @pl.when(i == 0)
    def _():
        s_ref[...] = s0_ref[...]

    mask, maskf = _chunk_masks(d)
    nega_r = -jnp.exp(al_r[...])
    nega_c = -jnp.exp(al_c[...])
    for cc in range(CPB):
        rows = _chunk_rows(d, cc)
        sm = sm_ref[rows, :]
        smt = sm.T
        dt_c = jax.nn.softplus(jnp.where(d == 0, sm[:, 32:36], sm[:, 36:40]) + db_r[...])
        dt_r = jax.nn.softplus(jnp.where(d == 0, smt[32:36, :], smt[36:40, :]) + db_c[...])
        a_c = dt_c * nega_r
        a_r = dt_r * nega_c
        cum_c = _mm_hi(maskf, a_c)
        cum_r = _mm_nt_hi(a_r, maskf)
        tot = jnp.sum(a_c, axis=0, keepdims=True)
        xs = x_ref[rows, :]
        bm = b_ref[rows, :]
        cm = c_ref[rows, :]
        cb = [_mm_nt(cm[:, g * 64:(g + 1) * 64], bm[:, g * 64:(g + 1) * 64]) for g in range(2)]
        outs = []
        for h in range(N_HEADS):
            g = h // 2
            sl = slice(h * HEAD_DIM, (h + 1) * HEAD_DIM)
            cmg = cm[:, g * 64:(g + 1) * 64]
            bmg = bm[:, g * 64:(g + 1) * 64]
            cc_h = cum_c[:, h:h + 1]
            decay = jnp.exp(jnp.where(mask, cc_h - cum_r[h:h + 1, :], -jnp.inf))
            xdt = xs[:, sl] * dt_c[:, h:h + 1]
            st = s_ref[h]
            outs.append(_mm(cb[g] * decay, xdt) + jnp.exp(cc_h) * _mm(cmg, st))
            tot_h = tot[:, h:h + 1]
            s_ref[h] = jnp.exp(tot_h) * st + _mm_tn(bmg * jnp.exp(tot_h - cc_h), xdt)
        o_ref[rows, :] = jnp.concatenate(outs, axis=-1)

    @pl.when(i == pl.num_programs(2) - 1)
    def _():
        sf_ref[...] = s_ref[...]


def _ssd(xbc, small, a_log, dt_bias, s0):
    bsz, length, _ = xbc.shape
    nb = length // SCAN_TB
    blk, out, state = _scan_specs(nb)
    return pl.pallas_call(
        _ssd_kernel, grid=(bsz, 2, nb),
        in_specs=[blk(GROUP_W, 0), blk(128, 2), blk(128, 3), blk(SMALL_W, 0),
                  _dir_spec((1, N_HEADS)), _dir_spec((N_HEADS, 1)),
                  _dir_spec((1, N_HEADS)), _dir_spec((N_HEADS, 1)), state],
        out_specs=[out, state],
        out_shape=[jax.ShapeDtypeStruct((2, bsz, length, GROUP_W), F32), jax.ShapeDtypeStruct(s0.shape, F32)],
        scratch_shapes=[pltpu.VMEM((N_HEADS, HEAD_DIM, HEAD_DIM), F32)],
        compiler_params=_params(("arbitrary",) * 3), name="ssd",
    )(xbc, xbc, xbc, small, a_log[:, None, :], a_log[:, :, None], dt_bias[:, None, :], dt_bias[:, :, None], s0)


def _mlstm_kernel(q_ref, k_ref, v_ref, sm_ref, gb_r, gb_c, c0_ref, nm0_ref, o_ref, cf_ref, nmf_ref,
                  c_ref, nm_ref):
    d = pl.program_id(1)
    i = pl.program_id(2)

    @pl.when(i == 0)
    def _():
        c_ref[...] = c0_ref[...]
        nm_ref[...] = nm0_ref[...]

    mask, maskf = _chunk_masks(d)
    rid = lax.broadcasted_iota(jnp.int32, (CHUNK, 1), 0)
    last = rid == jnp.where(d == 0, CHUNK - 1, 0)
    for cc in range(CPB):
        rows = _chunk_rows(d, cc)
        sm = sm_ref[rows, :]
        smt = sm.T
        gc = jnp.where(d == 0, sm[:, 40:48], sm[:, 48:56]) + gb_r[...]
        gr = jnp.where(d == 0, smt[40:48, :], smt[48:56, :]) + gb_c[...]
        i_c = gc[:, 0:4]
        i_r = gr[0:4, :]
        lf_c = jax.nn.log_sigmoid(gc[:, 4:8])
        lf_r = jax.nn.log_sigmoid(gr[4:8, :])
        fc_c = _mm_hi(maskf, lf_c)
        fc_r = _mm_nt_hi(lf_r, maskf)
        ftot = jnp.sum(lf_c, axis=0, keepdims=True)
        q = q_ref[rows, :]
        k = k_ref[rows, :] * (HEAD_DIM ** -0.5)
        v = v_ref[rows, :]
        outs = []
        for h in range(N_HEADS):
            sl = slice(h * HEAD_DIM, (h + 1) * HEAD_DIM)
            qh, kh, vh = q[:, sl], k[:, sl], v[:, sl]
            fc_h = fc_c[:, h:h + 1]
            li = jnp.where(mask, fc_h - fc_r[h:h + 1, :] + i_r[h:h + 1, :], -jnp.inf)
            m_s = nm_ref[4 + h:5 + h, 0:1]
            linter = fc_h + m_s
            m = jnp.maximum(linter, jnp.max(li, axis=-1, keepdims=True))
            w = _mm_nt(qh, kh) * jnp.exp(li - m)
            winter = jnp.exp(linter - m)
            cs = c_ref[h]
            ns = nm_ref[h:h + 1, 0:HEAD_DIM]
            num = _mm(w, vh) + winter * _mm(qh, cs)
            den = jnp.sum(w, axis=-1, keepdims=True) + winter * jnp.sum(qh * ns, axis=-1, keepdims=True)
            outs.append(num / jnp.maximum(jnp.abs(den), jnp.exp(-m)))
            m_new = jnp.sum(jnp.where(last, m, 0.0), axis=0, keepdims=True)
            ft_h = ftot[:, h:h + 1]
            decay = jnp.exp(ft_h + m_s - m_new)
            kw = kh * jnp.exp(ft_h - fc_h + i_c[:, h:h + 1] - m_new)
            c_ref[h] = decay * cs + _mm_tn(kw, vh)
            nm_ref[h:h + 1, 0:HEAD_DIM] = decay * ns + jnp.sum(kw, axis=0, keepdims=True)
            nm_ref[4 + h:5 + h, 0:1] = m_new
        o_ref[rows, :] = jnp.concatenate(outs, axis=-1)

    @pl.when(i == pl.num_programs(2) - 1)
    def _():
        cf_ref[...] = c_ref[...]
        nmf_ref[...] = nm_ref[...]


def _mlstm(qk, vo, small, gate_bias, c0, nm0):
    bsz, length, _ = qk.shape
    nb = length // SCAN_TB
    blk, out, state = _scan_specs(nb)
    nm_spec = pl.BlockSpec((None, None, 8, 128), lambda b, d, i: (b, d, 0, 0))
    gb = gate_bias.reshape(2, 1, 2 * N_HEADS)
    return pl.pallas_call(
        _mlstm_kernel, grid=(bsz, 2, nb),
        in_specs=[blk(GROUP_W, 0), blk(GROUP_W, 1), blk(GROUP_W, 0), blk(SMALL_W, 0),
                  _dir_spec((1, 2 * N_HEADS)), _dir_spec((2 * N_HEADS, 1)), state, nm_spec],
        out_specs=[out, state, nm_spec],
        out_shape=[jax.ShapeDtypeStruct((2, bsz, length, GROUP_W), F32), jax.ShapeDtypeStruct(c0.shape, F32),
                   jax.ShapeDtypeStruct(nm0.shape, F32)],
        scratch_shapes=[pltpu.VMEM((N_HEADS, HEAD_DIM, HEAD_DIM), F32), pltpu.VMEM((8, 128), F32)],
        compiler_params=_params(("arbitrary",) * 3), name="mlstm",
    )(qk, qk, vo, small, gb, gb.reshape(2, 2 * N_HEADS, 1), c0, nm0)


def _s5_prep_kernel(arc, aic, arr, air, lst, brn, bin_, bri, bii, ctr, cti, m_ref, q_ref, p_ref, a16r_ref, a16i_ref):
    n, t16 = S5_STATE, S5_T
    gw = S5_T * S5_CH
    lane_t = lax.broadcasted_iota(jnp.int32, (n, gw), 1) // S5_CH
    sel = (lax.broadcasted_iota(jnp.int32, (S5_CH, gw), 1) % S5_CH
           == lax.broadcasted_iota(jnp.int32, (S5_CH, gw), 0)).astype(F32)
    lane_blk = lax.broadcasted_iota(jnp.int32, (S5_CH, gw), 1) // S5_CH
    lane128 = lax.broadcasted_iota(jnp.int32, (S5_CH, 2 * n), 1)
    m_ref[...] = jnp.zeros_like(m_ref)
    q_ref[...] = jnp.zeros_like(q_ref)
    p_ref[...] = jnp.zeros_like(p_ref)

    def discretise(a_re, a_im, step):
        a_re = jnp.minimum(a_re, -1e-4)
        mag = jnp.exp(a_re * step)
        ab_re, ab_im = mag * jnp.cos(a_im * step), mag * jnp.sin(a_im * step)
        den = a_re * a_re + a_im * a_im
        nr, ni = ab_re - 1.0, ab_im
        return ab_re, ab_im, (nr * a_re + ni * a_im) / den, (ni * a_re - nr * a_im) / den

    def powers(ab_re, ab_im):
        pr, pi = [jnp.ones_like(ab_re)], [jnp.zeros_like(ab_re)]
        for _ in range(t16):
            pr.append(pr[-1] * ab_re - pi[-1] * ab_im)
            pi.append(pr[-2] * ab_im + pi[-1] * ab_re)
        return pr, pi

    for d in range(2):
        step_r = jnp.exp(jnp.concatenate([jnp.broadcast_to(lst[d, gl], (1, n)) for gl in range(2)], axis=1))
        ab_re, ab_im, cf_re, cf_im = discretise(arr[d], air[d], step_r)
        prr, pir = powers(ab_re, ab_im)
        a16r_ref[d] = prr[t16]
        a16i_ref[d] = pir[t16]
        bt_re = cf_re * bri[...] - cf_im * bii[...]
        bt_im = cf_re * bii[...] + cf_im * bri[...]
        for s in range(t16):
            e = t16 - 1 - s if d == 0 else s
            qre = bt_re * prr[e] - bt_im * pir[e]
            qim = bt_re * pir[e] + bt_im * prr[e]
            for gl in range(2):
                keep = (lane128 // n) == gl
                r0 = gl * gw + s * S5_CH
                q_ref[d, r0:r0 + S5_CH, 0:2 * n] = jnp.where(keep, qre, 0.0).astype(q_ref.dtype)
                q_ref[d, r0:r0 + S5_CH, 2 * n:4 * n] = jnp.where(keep, qim, 0.0).astype(q_ref.dtype)
        for gl in range(2):
            ab_re, ab_im, cf_re, cf_im = discretise(arc[d, gl], aic[d, gl], jnp.exp(lst[d, gl]))
            prc, pic = powers(ab_re, ab_im)
            bb_re = cf_re * brn[gl] - cf_im * bin_[gl]
            bb_im = cf_re * bin_[gl] + cf_im * brn[gl]
            ct_re = _mm_hi(ctr[gl], sel)
            ct_im = _mm_hi(cti[gl], sel)

            def response(exps):
                p_re = jnp.zeros((n, gw), F32)
                p_im = jnp.zeros((n, gw), F32)
                for t in range(t16):
                    p_re = jnp.where(lane_t == t, prc[exps[t]], p_re)
                    p_im = jnp.where(lane_t == t, pic[exps[t]], p_im)
                return ct_re * p_re - ct_im * p_im, ct_re * p_im + ct_im * p_re

            e_re, e_im = response([t if d == 0 else t16 - 1 - t for t in range(t16)])
            r0v = _mm_tn_hi(bb_re, e_re) - _mm_tn_hi(bb_im, e_im)
            for s in range(t16):
                if d == 0:
                    blk = jnp.where(lane_blk >= s, pltpu.roll(r0v, (S5_CH * s) % gw, 1), 0.0)
                else:
                    blk = jnp.where(lane_blk <= s, pltpu.roll(r0v, (gw - S5_CH * (t16 - 1 - s)) % gw, 1), 0.0)
                r0 = gl * gw + s * S5_CH
                m_ref[d, r0:r0 + S5_CH, gl * gw:(gl + 1) * gw] = blk.astype(m_ref.dtype)
            c_re, c_im = response([t + 1 if d == 0 else t16 - t for t in range(t16)])
            p_ref[d, gl * n:(gl + 1) * n, gl * gw:(gl + 1) * gw] = c_re.astype(p_ref.dtype)
            p_ref[d, 2 * n + gl * n:2 * n + (gl + 1) * n, gl * gw:(gl + 1) * gw] = (-c_im).astype(p_ref.dtype)


def _s5_prep(a_re, a_im, log_step, b_re, b_im, c_re, c_im):
    n, g, ch, pr = S5_STATE, S5_GROUPS, S5_CH, S5_PAIRS
    f = lambda t: t.astype(F32)
    arc, aic = f(a_re)[..., None], f(a_im)[..., None]
    arr, air = f(a_re).reshape(2, pr, 1, 2 * n), f(a_im).reshape(2, pr, 1, 2 * n)
    lst = f(log_step).reshape(2, g, 1, 1)
    pair_rows = lambda t: f(t).reshape(pr, 2, n, ch).transpose(0, 3, 1, 2).reshape(pr, ch, 2 * n)
    ctr, cti = f(c_re).transpose(0, 2, 1), f(c_im).transpose(0, 2, 1)
    gw2 = 2 * S5_T * S5_CH
    spec_c = pl.BlockSpec((2, 2, n, 1), lambda p: (0, p, 0, 0))
    spec_r = pl.BlockSpec((2, None, 1, 2 * n), lambda p: (0, p, 0, 0))
    spec_g = pl.BlockSpec((2, n, ch), lambda p: (p, 0, 0))
    spec_p = pl.BlockSpec((None, ch, 2 * n), lambda p: (p, 0, 0))
    outs = pl.pallas_call(
        _s5_prep_kernel, grid=(pr,),
        in_specs=[spec_c, spec_c, spec_r, spec_r, pl.BlockSpec((2, 2, 1, 1), lambda p: (0, p, 0, 0)),
                  spec_g, spec_g, spec_p, spec_p, spec_g, spec_g],
        out_specs=[pl.BlockSpec((2, None, gw2, gw2), lambda p: (0, p, 0, 0)),
                   pl.BlockSpec((2, None, gw2, 4 * n), lambda p: (0, p, 0, 0)),
                   pl.BlockSpec((2, None, 4 * n, gw2), lambda p: (0, p, 0, 0)),
                   pl.BlockSpec((2, None, 1, 2 * n), lambda p: (0, p, 0, 0)),
                   pl.BlockSpec((2, None, 1, 2 * n), lambda p: (0, p, 0, 0))],
        out_shape=[jax.ShapeDtypeStruct((2, pr, gw2, gw2), BF16), jax.ShapeDtypeStruct((2, pr, gw2, 4 * n), BF16),
                   jax.ShapeDtypeStruct((2, pr, 4 * n, gw2), BF16),
                   jax.ShapeDtypeStruct((2, pr, 1, 2 * n), F32), jax.ShapeDtypeStruct((2, pr, 1, 2 * n), F32)],
        compiler_params=_params(("arbitrary",)), name="s5_prep",
    )(arc, aic, arr, air, lst, f(b_re), f(b_im), pair_rows(b_re), pair_rows(b_im), ctr, cti)
    m, q, p, a16r, a16i = outs
    return m, q, p, a16r.reshape(2, 1, g * n), a16i.reshape(2, 1, g * n)


def _s5_state_kernel(u_ref, q_ref, vr_ref, vi_ref):
    v = jnp.dot(u_ref[...], q_ref[...], preferred_element_type=F32)
    vr_ref[...] = v[:, 0:128]
    vi_ref[...] = v[:, 128:256]


def _s5_scan_kernel(vr_ref, vi_ref, ar_ref, ai_ref, h0r_ref, h0i_ref, hr_ref, hi_ref, fr_ref, fi_ref, *, nch, bsz):
    d = pl.program_id(0)
    ar = jnp.broadcast_to(ar_ref[...], h0r_ref.shape)
    ai = jnp.broadcast_to(ai_ref[...], h0r_ref.shape)
    cpt = 8 // bsz
    ntile = nch // cpt

    def run(reverse):
        def body(kk, carry):
            hr, hi = carry
            t = ntile - 1 - kk if reverse else kk
            rows = pl.ds(pl.multiple_of(t * 8, 8), 8)
            vr8, vi8 = vr_ref[rows, :], vi_ref[rows, :]
            in_r, in_i = [None] * cpt, [None] * cpt
            for j in (range(cpt - 1, -1, -1) if reverse else range(cpt)):
                in_r[j], in_i[j] = hr, hi
                sl = slice(j * bsz, (j + 1) * bsz)
                hr, hi = ar * hr - ai * hi + vr8[sl], ar * hi + ai * hr + vi8[sl]
            hr_ref[rows, :] = jnp.concatenate(in_r, axis=0)
            hi_ref[rows, :] = jnp.concatenate(in_i, axis=0)
            return hr, hi

        hr, hi = lax.fori_loop(0, ntile, body, (h0r_ref[...], h0i_ref[...]))
        fr_ref[...] = hr
        fi_ref[...] = hi

    @pl.when(d == 0)
    def _():
        run(False)

    @pl.when(d == 1)
    def _():
        run(True)


def _s5_out_kernel(u_ref, m_ref, p_ref, hr_ref, hi_ref, y_ref):
    u = u_ref[...]
    y = None
    for d in range(2):
        hcat = jnp.concatenate([hr_ref[d], hi_ref[d]], axis=-1).astype(BF16)
        t = jnp.dot(u, m_ref[d], preferred_element_type=F32) + jnp.dot(hcat, p_ref[d], preferred_element_type=F32)
        y = t if y is None else y + t
    y_ref[...] = y


def _s5(ug, mats, h0r, h0i, bsz):
    m, q, p, a16r, a16i = mats
    rows = ug.shape[0]
    nch = rows // bsz
    gn = S5_GROUPS * S5_STATE
    gw2 = 2 * S5_T * S5_CH
    vr, vi = pl.pallas_call(
        _s5_state_kernel, grid=(2, S5_PAIRS),
        in_specs=[pl.BlockSpec((rows, gw2), lambda d, pp: (0, pp)),
                  pl.BlockSpec((None, None, gw2, 256), lambda d, pp: (d, pp, 0, 0))],
        out_specs=[pl.BlockSpec((None, rows, 128), lambda d, pp: (d, 0, pp))] * 2,
        out_shape=[jax.ShapeDtypeStruct((2, rows, gn), F32)] * 2,
        compiler_params=_params(("arbitrary", "arbitrary")), name="s5_state",
    )(ug, q)
    lb = 256
    blk = pl.BlockSpec((None, rows, lb), lambda d, j: (d, 0, j))
    vec = pl.BlockSpec((None, 1, lb), lambda d, j: (d, 0, j))
    st = pl.BlockSpec((None, bsz, lb), lambda d, j: (d, 0, j))
    hr, hi, fr, fi = pl.pallas_call(
        functools.partial(_s5_scan_kernel, nch=nch, bsz=bsz), grid=(2, gn // lb),
        in_specs=[blk, blk, vec, vec, st, st], out_specs=[blk, blk, st, st],
        out_shape=[jax.ShapeDtypeStruct((2, rows, gn), F32)] * 2 + [jax.ShapeDtypeStruct((2, bsz, gn), F32)] * 2,
        compiler_params=_params(("arbitrary", "arbitrary")), name="s5_scan",
    )(vr, vi, a16r, a16i, h0r, h0i)
    y = pl.pallas_call(
        _s5_out_kernel, grid=(S5_PAIRS,),
        in_specs=[pl.BlockSpec((rows, gw2), lambda pp: (0, pp)),
                  pl.BlockSpec((2, None, gw2, gw2), lambda pp: (0, pp, 0, 0)),
                  pl.BlockSpec((2, None, 256, gw2), lambda pp: (0, pp, 0, 0)),
                  pl.BlockSpec((2, rows, 128), lambda pp: (0, 0, pp)),
                  pl.BlockSpec((2, rows, 128), lambda pp: (0, 0, pp))],
        out_specs=pl.BlockSpec((rows, gw2), lambda pp: (0, pp)),
        out_shape=jax.ShapeDtypeStruct((rows, S5_GROUPS * S5_T * S5_CH), F32),
        compiler_params=_params(("arbitrary",)), name="s5_out",
    )(ug, m, p, hr, hi)
    return y, fr, fi


def _s5_to_chunks(u, lead):
    bsz = u.shape[0]
    nch = u.shape[1] // S5_T
    t = u.reshape(bsz, nch, S5_T, S5_GROUPS, S5_CH).transpose(1, 0, 3, 2, 4)
    return t.reshape(nch * bsz, S5_GROUPS * S5_T * S5_CH).astype(BF16)


def _s5_from_chunks(y, bsz):
    nch = y.shape[0] // bsz
    t = y.reshape(nch, bsz, S5_GROUPS, S5_T, S5_CH).transpose(1, 0, 3, 2, 4)
    return t.reshape(bsz, nch * S5_T, S5_GROUPS * S5_CH)


def _post_kernel(x_ref, gt_ref, go_ref, gr_ref, sy_ref, sx_ref, sz_ref, mh_ref, mo_ref, y5_ref, u5_ref,
                 gg_ref, sd_ref, sg_ref, mg_ref, d5_ref, wg_ref, bg_ref, avg_ref, wo_ref, o_ref, *, colmajor, rt):
    def rm(v):
        if not colmajor:
            return v
        return jnp.swapaxes(v, 0, 1).reshape(rt * GRID_W, v.shape[-1])

    def head_rms(t):
        return t * lax.rsqrt(_mm_hi(t * t, avg_ref[...]) + EPS)

    a = head_rms(go_ref[0] + go_ref[1]) * gg_ref[...] * jax.nn.silu(gr_ref[...])
    ys = rm(sy_ref[0] + sy_ref[1] + sd_ref[...] * sx_ref[...])
    b = _rms(ys * jax.nn.silu(rm(sz_ref[...]))) * sg_ref[...]
    m = head_rms(mh_ref[0] + mh_ref[1]) * mg_ref[...] * jax.nn.sigmoid(mo_ref[...])
    y5 = rm(y5_ref[...] + d5_ref[...] * u5_ref[...])
    gl = jax.nn.gelu(y5)
    dd = gl * jax.nn.sigmoid(_mm(gl, wg_ref[...]) + bg_ref[...])
    cat = jnp.concatenate([a, b, m, dd], axis=-1).astype(BF16)
    o_ref[...] = x_ref[...] + gt_ref[...] * jnp.dot(cat, wo_ref[...], preferred_element_type=F32)


def _post(x, modv, gla_o, gla_proj, ssd_y, ssd_xbc, ssd_z, ml_h, ml_vo, s5_y, s5_u, pw, colmajor):
    bsz, length, d = x.shape
    tm = min(512, length)
    rt = tm // GRID_W
    rows = length // GRID_W
    per_batch = modv.shape[0] > 1
    w = GROUP_W

    def rmspec(colblock, dirs=False):
        if dirs:
            return pl.BlockSpec((2, None, tm, w), lambda b, i: (0, b, i, colblock))
        return pl.BlockSpec((None, tm, w), lambda b, i: (b, i, colblock))

    def cmspec(colblock, dirs=False):
        if not colmajor:
            return rmspec(colblock, dirs)
        if dirs:
            return pl.BlockSpec((2, None, GRID_W, rt, w), lambda b, i: (0, b, 0, i, colblock))
        return pl.BlockSpec((None, GRID_W, rt, w), lambda b, i: (b, 0, i, colblock))

    def cm(t):
        return t.reshape(t.shape[:-2] + (GRID_W, rows, t.shape[-1])) if colmajor else t

    return pl.pallas_call(
        functools.partial(_post_kernel, colmajor=colmajor, rt=rt), grid=(bsz, length // tm),
        in_specs=[pl.BlockSpec((None, tm, d), lambda b, i: (b, i, 0)), _mod_spec(5, per_batch),
                  rmspec(0, True), rmspec(3), cmspec(0, True), cmspec(0), cmspec(0),
                  rmspec(0, True), rmspec(1), cmspec(0), cmspec(0),
                  _const((1, w)), _const((1, w)), _const((1, w)), _const((1, w)), _const((1, w)),
                  _const((w, w)), _const((1, w)), _const((w, w)), _const((4 * w, d))],
        out_specs=pl.BlockSpec((None, tm, d), lambda b, i: (b, i, 0)),
        out_shape=jax.ShapeDtypeStruct(x.shape, F32),
        compiler_params=_params(("arbitrary", "arbitrary")), name="post",
    )(x, modv, gla_o, gla_proj, cm(ssd_y), cm(ssd_xbc), cm(ssd_z), ml_h, ml_vo, cm(s5_y), cm(s5_u),
      pw["gla_g"], pw["ssd_d"], pw["ssd_g"], pw["ml_g"], pw["s5_d"], pw["w_glu"], pw["b_glu"], pw["avg"],
      pw["w_out"])


def _token_mixing(h_in, modv, g1, lw, states, colmajor):
    bsz, length, _ = h_in.shape
    pr = _inproj(h_in, modv, g1, lw["w_in"], colmajor)
    xbc = _conv_silu(pr["ssd_xbc"], lw["ssd_conv_w"], lw["ssd_conv_b"])
    mqk = _conv_silu(pr["ml_qk"], lw["ml_conv_w"], lw["ml_conv_b"])
    gla_o, gla_s = _gla(pr["gla"], pr["small"], lw["gla_w_lr2"], lw["gla_b_lr2"], states["gla"])
    ssd_y, ssd_s = _ssd(xbc, pr["small_cm"], lw["ssd_a_log"], lw["ssd_dt_bias"], states["ssd"])
    ml_h, ml_c, ml_nm = _mlstm(mqk, pr["ml_vo"], pr["small"], lw["ml_gate_bias"], states["ml_c"], states["ml_nm"])
    y5, s5r, s5i = _s5(_s5_to_chunks(pr["s5"], None), lw["s5_mats"], states["s5_r"], states["s5_i"], bsz)
    outs = dict(gla_o=gla_o, gla_proj=pr["gla"], ssd_y=ssd_y, ssd_xbc=xbc, ssd_z=pr["ssd_z"], ml_h=ml_h,
                ml_vo=pr["ml_vo"], s5_y=_s5_from_chunks(y5, bsz), s5_u=pr["s5"])
    finals = dict(gla=gla_s, ssd=ssd_s, ml_c=ml_c, ml_nm=ml_nm, s5_r=s5r, s5_i=s5i)
    return outs, finals


def _proj_weight(w):
    cols = [w[:, 0:1024], w[:, 1056:1568], w[:, 1568:1824], w[:, 1832:2344], w[:, 2344:2856], w[:, 2872:3128],
            w[:, 1024:1056], w[:, 1824:1832], w[:, 2856:2872], jnp.zeros((w.shape[0], SMALL_W - 56), w.dtype)]
    return jnp.concatenate(cols, axis=1).astype(BF16)


def kernel(x, c, ctx, c_ctx, w_mod, b_mod, g_norm, ffn_w_in, ffn_w_out, w_in, w_out, gla_w_lr2, gla_b_lr2,
           gla_g_norm, ssd_conv_w, ssd_conv_b, ssd_a_log, ssd_dt_bias, ssd_d, ssd_g_norm, ml_conv_w, ml_conv_b,
           ml_gate_bias, ml_g_norm, s5_a_re, s5_a_im, s5_log_step, s5_b_re, s5_b_im, s5_c_re, s5_c_im, s5_d,
           s5_w_glu, s5_b_glu, g_final):
    bsz, length, d = x.shape
    depth = w_mod.shape[0]
    assert bsz + 1 <= MOD_ROWS and d == D_MODEL and length % 1024 == 0 and ctx.shape[1] % SCAN_TB == 0
    cc = jnp.concatenate([c, c_ctx[None, :], jnp.zeros((MOD_ROWS - bsz - 1, d), F32)], axis=0)
    mod = _modulation(cc, w_mod, b_mod)
    avg = jnp.kron(jnp.eye(N_HEADS, dtype=F32), jnp.full((HEAD_DIM, HEAD_DIM), 1.0 / HEAD_DIM, F32))
    row = lambda t: t.reshape(1, -1).astype(F32)
    zero_states = dict(
        gla=jnp.zeros((bsz, 2, N_HEADS, HEAD_DIM, HEAD_DIM), F32), ssd=jnp.zeros((bsz, 2, N_HEADS, HEAD_DIM, HEAD_DIM), F32),
        ml_c=jnp.zeros((bsz, 2, N_HEADS, HEAD_DIM, HEAD_DIM), F32), ml_nm=jnp.zeros((bsz, 2, 8, 128), F32),
        s5_r=jnp.zeros((2, bsz, S5_GROUPS * S5_STATE), F32), s5_i=jnp.zeros((2, bsz, S5_GROUPS * S5_STATE), F32))
    for l in range(depth):
        last = l == depth - 1
        mod_x = mod[l, :bsz].reshape(bsz, N_MOD, 1, d)
        mod_c = mod[l, bsz:bsz + 1].reshape(1, N_MOD, 1, d)
        wi = [ffn_w_in[l, j].astype(BF16) for j in range(2)]
        wo = [ffn_w_out[l, j].astype(BF16) for j in range(2)]
        g = [row(g_norm[l, j]) for j in range(3)]
        lw = dict(
            w_in=_proj_weight(w_in[l]), gla_w_lr2=gla_w_lr2[l], gla_b_lr2=gla_b_lr2[l][:, None, :],
            ssd_conv_w=ssd_conv_w[l], ssd_conv_b=row(ssd_conv_b[l]), ssd_a_log=ssd_a_log[l].astype(F32),
            ssd_dt_bias=ssd_dt_bias[l], ml_conv_w=ml_conv_w[l], ml_conv_b=row(ml_conv_b[l]),
            ml_gate_bias=ml_gate_bias[l],
            s5_mats=_s5_prep(s5_a_re[l], s5_a_im[l], s5_log_step[l], s5_b_re[l], s5_b_im[l], s5_c_re[l], s5_c_im[l]))
        pw = dict(gla_g=row(gla_g_norm[l]), ssd_d=row(jnp.repeat(ssd_d[l], HEAD_DIM)), ssd_g=row(ssd_g_norm[l]),
                  ml_g=row(ml_g_norm[l]), s5_d=row(s5_d[l]), w_glu=s5_w_glu[l].astype(BF16), b_glu=row(s5_b_glu[l]),
                  avg=avg, w_out=w_out[l].astype(BF16))
        x = _ffn(x, mod_x, 0, g[0], wi[0], wo[0])
        ctx = _ffn(ctx, mod_c, 0, g[0], wi[0], wo[0])
        outs_c, finals = _token_mixing(ctx, mod_c, g[1], lw, zero_states, colmajor=False)
        outs_x, _ = _token_mixing(x, mod_x, g[1], lw, finals, colmajor=True)
        x = _post(x, mod_x, pw=pw, colmajor=True, **outs_x)
        x = _ffn(x, mod_x, 2, g[2], wi[1], wo[1], g_final=row(g_final) if last else None)
        if not last:
            ctx = _post(ctx, mod_c, pw=pw, colmajor=False, **outs_c)
            ctx = _ffn(ctx, mod_c, 2, g[2], wi[1], wo[1])
    return x
```

```python
import functools

import jax
import jax.numpy as jnp
from jax import lax
from jax.experimental import pallas as pl
from jax.experimental.pallas import tpu as pltpu

F32 = jnp.float32
BF16 = jnp.bfloat16
HI = lax.Precision.HIGHEST

D_MODEL = 1024
D_FF = 2816
GRID_W = 64
GROUP_W = 256
HEAD_DIM = 64
N_HEADS = 4
CHUNK = 64
EPS = 1e-6
N_MOD = 9
GLA_RANK = 16
GLA_NORMALISER = 16.0
S5_CH = 16
S5_GROUPS = 16
S5_STATE = 64
S5_T = 16
S5_PAIRS = S5_GROUPS // 2
SMALL_W = 128
PROJ_W = 3072 + SMALL_W
MOD_ROWS = 8
VMEM_LIMIT = 56 * 1024 * 1024


def _params(sem, vmem=VMEM_LIMIT):
    return pltpu.CompilerParams(dimension_semantics=sem, vmem_limit_bytes=vmem)


def _mm(a, b):
    return jnp.dot(a.astype(BF16), b.astype(BF16), preferred_element_type=F32)


def _mm_nt(a, b):
    return lax.dot_general(a.astype(BF16), b.astype(BF16), (((1,), (1,)), ((), ())), preferred_element_type=F32)


def _mm_tn(a, b):
    return lax.dot_general(a.astype(BF16), b.astype(BF16), (((0,), (0,)), ((), ())), preferred_element_type=F32)


def _mm_hi(a, b):
    return jnp.dot(a, b, precision=HI, preferred_element_type=F32)


def _mm_tn_hi(a, b):
    return lax.dot_general(a, b, (((0,), (0,)), ((), ())), precision=HI, preferred_element_type=F32)


def _rms(t):
    return t * lax.rsqrt(jnp.mean(t * t, axis=-1, keepdims=True) + EPS)


def _modnorm(xv, g, shift, scale):
    return (_rms(xv) * g) * (1.0 + scale) + shift


def _const(shape):
    n = len(shape)
    return pl.BlockSpec(shape, lambda *_: (0,) * n, pipeline_mode=pl.Buffered(1))


def _mod_kernel(c_ref, w_ref, b_ref, o_ref):
    o_ref[...] = _mm_hi(jax.nn.silu(c_ref[...]), w_ref[...]) + b_ref[...]


def _modulation(cc, w_mod, b_mod):
    depth, d, n = w_mod.shape
    tn = 1024
    return pl.pallas_call(
        _mod_kernel, grid=(depth, n // tn),
        in_specs=[pl.BlockSpec((MOD_ROWS, d), lambda l, j: (0, 0)),
                  pl.BlockSpec((None, d, tn), lambda l, j: (l, 0, j)),
                  pl.BlockSpec((None, 1, tn), lambda l, j: (l, 0, j))],
        out_specs=pl.BlockSpec((None, MOD_ROWS, tn), lambda l, j: (l, 0, j)),
        out_shape=jax.ShapeDtypeStruct((depth, MOD_ROWS, n), F32),
        compiler_params=_params(("arbitrary", "arbitrary")), name="modulation",
    )(cc, w_mod, b_mod.reshape(depth, 1, n))


def _mod_spec(k, per_batch):
    if per_batch:
        return pl.BlockSpec((None, None, 1, D_MODEL), lambda b, i: (b, k, 0, 0))
    return pl.BlockSpec((None, None, 1, D_MODEL), lambda b, i: (0, k, 0, 0))


def _ffn_kernel(x_ref, sh_ref, sc_ref, gt_ref, g_ref, wi_ref, wo_ref, *rest, final):
    o_ref = rest[-1]
    xv = x_ref[...]
    h = _modnorm(xv, g_ref[...], sh_ref[...], sc_ref[...]).astype(BF16)
    gu = jnp.dot(h, wi_ref[...], preferred_element_type=F32)
    a = (jax.nn.silu(gu[:, :D_FF]) * gu[:, D_FF:]).astype(BF16)
    y = jnp.dot(a, wo_ref[...], preferred_element_type=F32)
    out = xv + (0.5 * gt_ref[...]) * y
    if final:
        out = _rms(out) * rest[0][...]
    o_ref[...] = out


def _ffn(x, modv, j, g, w_in, w_out, g_final=None):
    bsz, length, d = x.shape
    tm = min(512, length)
    per_batch = modv.shape[0] > 1
    final = g_final is not None
    ins = [x, modv, modv, modv, g, w_in, w_out]
    specs = [pl.BlockSpec((None, tm, d), lambda b, i: (b, i, 0)),
             _mod_spec(3 * j, per_batch), _mod_spec(3 * j + 1, per_batch), _mod_spec(3 * j + 2, per_batch),
             _const((1, d)), _const((d, 2 * D_FF)), _const((D_FF, d))]
    if final:
        ins.append(g_final)
        specs.append(_const((1, d)))
    return pl.pallas_call(
        functools.partial(_ffn_kernel, final=final), grid=(bsz, length // tm),
        in_specs=specs, out_specs=pl.BlockSpec((None, tm, d), lambda b, i: (b, i, 0)),
        out_shape=jax.ShapeDtypeStruct(x.shape, F32),
        compiler_params=_params(("arbitrary", "arbitrary")), name="ffn",
    )(*ins)


_PROJ_SPLIT = (("gla", 0, 1024, False), ("ml_qk", 1792, 2304, False), ("ml_vo", 2304, 2816, False),
               ("small", 3072, 3200, False), ("ssd_xbc", 1024, 1536, True), ("ssd_z", 1536, 1792, True),
               ("s5", 2816, 3072, True), ("small_cm", 3072, 3200, True))


def _inproj_kernel(x_ref, sh_ref, sc_ref, g_ref, w_ref, *outs, colmajor, rt):
    h = _modnorm(x_ref[...], g_ref[...], sh_ref[...], sc_ref[...]).astype(BF16)
    y = jnp.dot(h, w_ref[...], preferred_element_type=F32)
    for o_ref, (_, lo, hi, cm) in zip(outs, _PROJ_SPLIT):
        v = y[:, lo:hi]
        if cm and colmajor:
            v = jnp.swapaxes(v.reshape(rt, GRID_W, hi - lo), 0, 1)
        o_ref[...] = v


def _inproj(x, modv, g, w, colmajor):
    bsz, length, d = x.shape
    tm = min(512, length)
    rt = tm // GRID_W
    rows = length // GRID_W
    per_batch = modv.shape[0] > 1
    shapes, specs = [], []
    for _, lo, hi, cm in _PROJ_SPLIT:
        n = hi - lo
        if cm and colmajor:
            shapes.append(jax.ShapeDtypeStruct((bsz, GRID_W, rows, n), F32))
            specs.append(pl.BlockSpec((None, GRID_W, rt, n), lambda b, i: (b, 0, i, 0)))
        else:
            shapes.append(jax.ShapeDtypeStruct((bsz, length, n), F32))
            specs.append(pl.BlockSpec((None, tm, n), lambda b, i: (b, i, 0)))
    outs = pl.pallas_call(
        functools.partial(_inproj_kernel, colmajor=colmajor, rt=rt), grid=(bsz, length // tm),
        in_specs=[pl.BlockSpec((None, tm, d), lambda b, i: (b, i, 0)),
                  _mod_spec(3, per_batch), _mod_spec(4, per_batch), _const((1, d)), _const((d, PROJ_W))],
        out_specs=specs, out_shape=shapes,
        compiler_params=_params(("arbitrary", "arbitrary")), name="inproj",
    )(x, modv, modv, g, w)
    res = {}
    for o, (name, lo, hi, cm) in zip(outs, _PROJ_SPLIT):
        res[name] = o.reshape(bsz, length, hi - lo)
    return res


def _conv_kernel(x_ref, p_ref, n_ref, w_ref, b_ref, o_ref, *, tb):
    i = pl.program_id(1)
    nb = pl.num_programs(1)
    x = x_ref[...]
    rid = lax.broadcasted_iota(jnp.int32, x.shape, 0)
    prev = jnp.where(i > 0, p_ref[7:8, :], 0.0)
    nxt = jnp.where(i < nb - 1, n_ref[0:1, :], 0.0)
    xm = jnp.where(rid == 0, prev, pltpu.roll(x, 1, 0))
    xp = jnp.where(rid == tb - 1, nxt, pltpu.roll(x, tb - 1, 0))
    y = w_ref[0:1, :] * xm + w_ref[1:2, :] * x + w_ref[2:3, :] * xp + b_ref[...]
    o_ref[...] = jax.nn.silu(y)


def _conv_silu(x, w, b):
    bsz, length, ch = x.shape
    tb = min(1024, length)
    t8 = tb // 8
    last8 = length // 8 - 1
    return pl.pallas_call(
        functools.partial(_conv_kernel, tb=tb), grid=(bsz, length // tb),
        in_specs=[pl.BlockSpec((None, tb, ch), lambda bb, i: (bb, i, 0)),
                  pl.BlockSpec((None, 8, ch), lambda bb, i: (bb, jnp.maximum(i * t8 - 1, 0), 0)),
                  pl.BlockSpec((None, 8, ch), lambda bb, i: (bb, jnp.minimum((i + 1) * t8, last8), 0)),
                  _const((3, ch)), _const((1, ch))],
        out_specs=pl.BlockSpec((None, tb, ch), lambda bb, i: (bb, i, 0)),
        out_shape=jax.ShapeDtypeStruct(x.shape, F32),
        compiler_params=_params(("arbitrary", "arbitrary")), name="conv_silu",
    )(x, x, x, w, b)


HW = N_HEADS * HEAD_DIM
CPB = 2
SCAN_TB = CPB * CHUNK


def _scan_block(d, i, nb):
    return jnp.where(d == 0, i, nb - 1 - i)


def _chunk_rows(d, cc):
    c = jnp.where(d == 0, cc, CPB - 1 - cc)
    return pl.ds(pl.multiple_of(c * CHUNK, CHUNK), CHUNK)


def _scan_consts(d):
    t = lax.broadcasted_iota(jnp.int32, (CHUNK, HW), 0)
    lane = lax.broadcasted_iota(jnp.int32, (CHUNK, HW), 1)
    s = jnp.bitwise_and(lane, HEAD_DIM - 1)
    vis = jnp.where(d == 0, t - s, s - t)
    r64 = lax.broadcasted_iota(jnp.int32, (CHUNK, CHUNK), 0)
    c64 = lax.broadcasted_iota(jnp.int32, (CHUNK, CHUNK), 1)
    rb = lax.broadcasted_iota(jnp.int32, (HW, HW), 0) // HEAD_DIM
    cb = lax.broadcasted_iota(jnp.int32, (HW, HW), 1) // HEAD_DIM
    return dict(
        mask4=vis >= 0,
        maskt4=(vis <= 0).astype(F32),
        eye4=(vis == 0).astype(F32),
        cumsum=(jnp.where(d == 0, r64 - c64, c64 - r64) >= 0).astype(BF16),
        bd=(rb == cb).astype(BF16),
        head=lane // HEAD_DIM)


def _split(x, terms):
    parts = []
    for _ in range(terms - 1):
        parts.append(x.astype(BF16))
        x = x - parts[-1].astype(F32)
    return parts + [x.astype(BF16)]


def _mm_r01(x, w01, terms=2):
    return sum(jnp.dot(p, w01, preferred_element_type=F32) for p in _split(x, terms))


def _mm_l01(w01, x, terms=3):
    return sum(jnp.dot(w01, p, preferred_element_type=F32) for p in _split(x, terms))


def _bd(x, bd01):
    return jnp.concatenate([x.astype(BF16)] * N_HEADS, axis=0) * bd01


def _scan_specs(nb, bsz):
    def blk(width, colblock):
        return pl.BlockSpec((bsz, SCAN_TB, width), lambda d, i: (0, _scan_block(d, i, nb), colblock))
    out = pl.BlockSpec((None, bsz, SCAN_TB, GROUP_W), lambda d, i: (d, 0, _scan_block(d, i, nb), 0))
    state = pl.BlockSpec((bsz, None, HW, HW), lambda d, i: (0, d, 0, 0))
    return blk, out, state


def _dir_spec(shape):
    n = len(shape)
    return pl.BlockSpec((None,) + shape, lambda d, i: (d,) + (0,) * n)


def _gla_kernel(q_ref, k_ref, v_ref, sm_ref, w_ref, b_ref, s0_ref, o_ref, sf_ref, s_ref, *, bsz):
    d = pl.program_id(0)
    i = pl.program_id(1)

    @pl.when(i == 0)
    def _():
        s_ref[...] = s0_ref[...]

    cs = _scan_consts(d)
    bs = range(bsz)
    for cc in range(CPB):
        rows = _chunk_rows(d, cc)
        sm = sm_ref[:, rows, :].reshape(bsz * CHUNK, SMALL_W)
        lr = jnp.where(d == 0, sm[:, 0:GLA_RANK], sm[:, GLA_RANK:2 * GLA_RANK])
        g = jax.nn.log_sigmoid(_mm(lr, w_ref[...]) + b_ref[...]) * (1.0 / GLA_NORMALISER)
        g = [g[b * CHUNK:(b + 1) * CHUNK] for b in bs]
        bc = _mm_l01(cs["cumsum"], jnp.concatenate(g, axis=1))
        bc = [bc[:, b * HW:(b + 1) * HW] for b in bs]
        bt = [jnp.sum(g[b], axis=0, keepdims=True) for b in bs]
        k = [k_ref[b, rows, :] for b in bs]
        v = [v_ref[b, rows, :].astype(BF16) for b in bs]
        qt = [(q_ref[b, rows, :] * (HEAD_DIM ** -0.5) * jnp.exp(bc[b])).astype(BF16) for b in bs]
        st = [s_ref[b] for b in bs]
        o_st = [_mm_nt(qt[b], st[b].astype(BF16) * cs["bd"]) for b in bs]
        kbd = [_bd(k[b] * jnp.exp(-bc[b]), cs["bd"]) for b in bs]
        att = [jnp.where(cs["mask4"], _mm_nt(qt[b], kbd[b]), 0.0) for b in bs]
        ks = [(k[b] * jnp.exp(bt[b] - bc[b])).astype(BF16) for b in bs]
        upd = [_mm_tn(v[b], ks[b]) for b in bs]
        vbd = [_bd(v[b], cs["bd"]) for b in bs]
        for b in bs:
            o_ref[b, rows, :] = _mm(att[b], vbd[b]) + o_st[b]
        for b in bs:
            s_ref[b] = st[b] * jnp.exp(bt[b]) + upd[b]

    @pl.when(i == pl.num_programs(1) - 1)
    def _():
        sf_ref[...] = s_ref[...]


def _gla(proj, small, w_lr2, b_lr2, s0):
    bsz, length, _ = proj.shape
    nb = length // SCAN_TB
    blk, out, state = _scan_specs(nb, bsz)
    return pl.pallas_call(
        functools.partial(_gla_kernel, bsz=bsz), grid=(2, nb),
        in_specs=[blk(GROUP_W, 0), blk(GROUP_W, 1), blk(GROUP_W, 2), blk(SMALL_W, 0),
                  _dir_spec((GLA_RANK, GROUP_W)), _dir_spec((1, GROUP_W)), state],
        out_specs=[out, state],
        out_shape=[jax.ShapeDtypeStruct((2, bsz, length, GROUP_W), F32), jax.ShapeDtypeStruct(s0.shape, F32)],
        scratch_shapes=[pltpu.VMEM((bsz, HW, HW), F32)],
        compiler_params=_params(("arbitrary",) * 2), name="gla",
    )(proj, proj, proj, small, w_lr2, b_lr2, s0)


def _ssd_kernel(x_ref, b_ref, c_ref, sm_ref, al_ref, db_ref, s0_ref, o_ref, sf_ref, s_ref, *, bsz):
    d = pl.program_id(0)
    i = pl.program_id(1)

    @pl.when(i == 0)
    def _():
        s_ref[...] = s0_ref[...]

    cs = _scan_consts(d)
    r = lax.broadcasted_iota(jnp.int32, (SMALL_W, HW), 0)
    c = lax.broadcasted_iota(jnp.int32, (SMALL_W, HW), 1)
    e_dt = (r == 32 + N_HEADS * d + c // HEAD_DIM).astype(BF16)
    e_grp = ((r // HEAD_DIM == c // (2 * HEAD_DIM)) & (r % HEAD_DIM == c % HEAD_DIM)).astype(BF16)
    rg = lax.broadcasted_iota(jnp.int32, (HW, 2 * HEAD_DIM), 0) // (2 * HEAD_DIM)
    cg = lax.broadcasted_iota(jnp.int32, (HW, 2 * HEAD_DIM), 1) // HEAD_DIM
    grp = (rg == cg).astype(BF16)
    nega = -jnp.exp(al_ref[...])
    bs = range(bsz)
    for cc in range(CPB):
        rows = _chunk_rows(d, cc)
        sm = sm_ref[:, rows, :].reshape(bsz * CHUNK, SMALL_W)
        dt = jax.nn.softplus(_mm_r01(sm, e_dt) + db_ref[...])
        a = dt * nega
        dt = [dt[b * CHUNK:(b + 1) * CHUNK] for b in bs]
        a = [a[b * CHUNK:(b + 1) * CHUNK] for b in bs]
        cum = _mm_l01(cs["cumsum"], jnp.concatenate(a, axis=1))
        cum = [cum[:, b * HW:(b + 1) * HW] for b in bs]
        bm = b_ref[:, rows, :].reshape(bsz * CHUNK, 2 * HEAD_DIM).astype(BF16)
        cm = c_ref[:, rows, :].reshape(bsz * CHUNK, 2 * HEAD_DIM).astype(BF16)
        cmx = _mm(cm, e_grp)
        bmx = _mm(bm, e_grp)
        bm = [bm[b * CHUNK:(b + 1) * CHUNK] for b in bs]
        cm = [cm[b * CHUNK:(b + 1) * CHUNK] for b in bs]
        st = [s_ref[b] for b in bs]
        y_st = [_mm(cmx[b * CHUNK:(b + 1) * CHUNK], st[b].astype(BF16) * cs["bd"]) for b in bs]
        cb4 = [_mm_nt(cm[b], jnp.concatenate([bm[b]] * N_HEADS, axis=0) * grp) for b in bs]
        tot = [jnp.sum(a[b], axis=0, keepdims=True) for b in bs]
        cumr = [jnp.sum(a[b] * cs["maskt4"], axis=0, keepdims=True) for b in bs]
        xdt = [(x_ref[b, rows, :] * dt[b]).astype(BF16) for b in bs]
        bw = [(bmx[b * CHUNK:(b + 1) * CHUNK] * jnp.exp(tot[b] - cum[b])).astype(BF16) for b in bs]
        upd = [_mm_tn(bw[b], xdt[b]) for b in bs]
        xbd = [_bd(xdt[b], cs["bd"]) for b in bs]
        sc = [cb4[b] * jnp.exp(jnp.where(cs["mask4"], cum[b] - cumr[b], -jnp.inf)) for b in bs]
        for b in bs:
            o_ref[b, rows, :] = _mm(sc[b], xbd[b]) + jnp.exp(cum[b]) * y_st[b]
        for b in bs:
            s_ref[b] = st[b] * jnp.exp(tot[b]) + upd[b]

    @pl.when(i == pl.num_programs(1) - 1)
    def _():
        sf_ref[...] = s_ref[...]


def _per_head_lanes(t):
    return jnp.repeat(t.astype(F32), HEAD_DIM, axis=-1)[:, None, :]


def _ssd(xbc, small, a_log, dt_bias, s0):
    bsz, length, _ = xbc.shape
    nb = length // SCAN_TB
    blk, out, state = _scan_specs(nb, bsz)
    return pl.pallas_call(
        functools.partial(_ssd_kernel, bsz=bsz), grid=(2, nb),
        in_specs=[blk(GROUP_W, 0), blk(128, 2), blk(128, 3), blk(SMALL_W, 0),
                  _dir_spec((1, HW)), _dir_spec((1, HW)), state],
        out_specs=[out, state],
        out_shape=[jax.ShapeDtypeStruct((2, bsz, length, GROUP_W), F32), jax.ShapeDtypeStruct(s0.shape, F32)],
        scratch_shapes=[pltpu.VMEM((bsz, HW, HW), F32)],
        compiler_params=_params(("arbitrary",) * 2), name="ssd",
    )(xbc, xbc, xbc, small, _per_head_lanes(a_log), _per_head_lanes(dt_bias), s0)


def _mlstm_kernel(q_ref, k_ref, v_ref, sm_ref, gi_ref, gf_ref, c0_ref, nm0_ref, o_ref, cf_ref, nmf_ref,
                  c_ref, nm_ref, *, bsz):
    d = pl.program_id(0)
    i = pl.program_id(1)

    @pl.when(i == 0)
    def _():
        c_ref[...] = c0_ref[...]
        nm_ref[...] = nm0_ref[...]

    cs = _scan_consts(d)
    r = lax.broadcasted_iota(jnp.int32, (SMALL_W, 2 * HW), 0)
    c = lax.broadcasted_iota(jnp.int32, (SMALL_W, 2 * HW), 1)
    gate = 40 + 2 * N_HEADS * d + N_HEADS * (c // HW) + (c % HW) // HEAD_DIM
    e_if = (r == gate).astype(BF16)
    rid = lax.broadcasted_iota(jnp.int32, (CHUNK, 1), 0)
    last = rid == jnp.where(d == 0, CHUNK - 1, 0)
    bs = range(bsz)

    def per_b(t):
        return [t[b * CHUNK:(b + 1) * CHUNK] for b in bs]

    for cc in range(CPB):
        rows = _chunk_rows(d, cc)
        sm = sm_ref[:, rows, :].reshape(bsz * CHUNK, SMALL_W)
        gates = _mm_r01(sm, e_if)
        gi = per_b(gates[:, :HW] + gi_ref[...])
        lf = per_b(jax.nn.log_sigmoid(gates[:, HW:] + gf_ref[...]))
        fc = _mm_l01(cs["cumsum"], jnp.concatenate(lf, axis=1))
        fc = [fc[:, b * HW:(b + 1) * HW] for b in bs]
        q = [q_ref[b, rows, :] for b in bs]
        qb = [q[b].astype(BF16) for b in bs]
        k = [k_ref[b, rows, :] * (HEAD_DIM ** -0.5) for b in bs]
        v = [v_ref[b, rows, :].astype(BF16) for b in bs]
        cst = [c_ref[b] for b in bs]
        ns = [nm_ref[b, 0:1, :] for b in bs]
        ms = [nm_ref[b, 1:2, :] for b in bs]
        n_st = [_mm(qb[b], cst[b].astype(BF16) * cs["bd"]) for b in bs]
        d_st = [_mm_r01(q[b] * ns[b], cs["bd"]) for b in bs]
        qk = [_mm_nt(qb[b], _bd(k[b], cs["bd"])) for b in bs]
        ftot = [jnp.sum(lf[b], axis=0, keepdims=True) for b in bs]
        fr = [jnp.sum(lf[b] * cs["maskt4"], axis=0, keepdims=True) for b in bs]
        ir = [jnp.sum(gi[b] * cs["eye4"], axis=0, keepdims=True) for b in bs]
        li = [jnp.where(cs["mask4"], fc[b] - fr[b] + ir[b], -jnp.inf) for b in bs]
        m = []
        for b in bs:
            rmax = jnp.zeros_like(li[b])
            for h in range(N_HEADS):
                mh = jnp.max(li[b][:, h * HEAD_DIM:(h + 1) * HEAD_DIM], axis=-1, keepdims=True)
                rmax = jnp.where(cs["head"] == h, mh, rmax)
            m.append(jnp.maximum(fc[b] + ms[b], rmax))
        m_new = [jnp.sum(jnp.where(last, m[b], 0.0), axis=0, keepdims=True) for b in bs]
        kw = [(k[b] * jnp.exp(ftot[b] - fc[b] + gi[b] - m_new[b])) for b in bs]
        upd = [_mm_tn(kw[b], v[b]) for b in bs]
        wf = [qk[b] * jnp.exp(li[b] - m[b]) for b in bs]
        w = [wf[b].astype(BF16) for b in bs]
        w_lo = [(wf[b] - w[b].astype(F32)).astype(BF16) for b in bs]
        vbd = [_bd(v[b], cs["bd"]) for b in bs]
        n_in = [_mm(w[b], vbd[b]) for b in bs]
        d_in = [_mm(w[b], cs["bd"]) + _mm(w_lo[b], cs["bd"]) for b in bs]
        for b in bs:
            winter = jnp.exp(fc[b] + ms[b] - m[b])
            den = d_in[b] + winter * d_st[b]
            o_ref[b, rows, :] = (n_in[b] + winter * n_st[b]) / jnp.maximum(jnp.abs(den), jnp.exp(-m[b]))
        for b in bs:
            decay = jnp.exp(ftot[b] + ms[b] - m_new[b])
            c_ref[b] = cst[b] * decay + upd[b]
            nm_ref[b, 0:1, :] = decay * ns[b] + jnp.sum(kw[b], axis=0, keepdims=True)
            nm_ref[b, 1:2, :] = m_new[b]

    @pl.when(i == pl.num_programs(1) - 1)
    def _():
        cf_ref[...] = c_ref[...]
        nmf_ref[...] = nm_ref[...]


def _mlstm(qk, vo, small, gate_bias, c0, nm0):
    bsz, length, _ = qk.shape
    nb = length // SCAN_TB
    blk, out, state = _scan_specs(nb, bsz)
    nm_spec = pl.BlockSpec((bsz, None, 8, HW), lambda d, i: (0, d, 0, 0))
    return pl.pallas_call(
        functools.partial(_mlstm_kernel, bsz=bsz), grid=(2, nb),
        in_specs=[blk(GROUP_W, 0), blk(GROUP_W, 1), blk(GROUP_W, 0), blk(SMALL_W, 0),
                  _dir_spec((1, HW)), _dir_spec((1, HW)), state, nm_spec],
        out_specs=[out, state, nm_spec],
        out_shape=[jax.ShapeDtypeStruct((2, bsz, length, GROUP_W), F32), jax.ShapeDtypeStruct(c0.shape, F32),
                   jax.ShapeDtypeStruct(nm0.shape, F32)],
        scratch_shapes=[pltpu.VMEM((bsz, HW, HW), F32), pltpu.VMEM((bsz, 8, HW), F32)],
        compiler_params=_params(("arbitrary",) * 2), name="mlstm",
    )(qk, qk, vo, small, _per_head_lanes(gate_bias[:, 0]), _per_head_lanes(gate_bias[:, 1]), c0, nm0)


def _s5_prep_kernel(arc, aic, arr, air, lst, brn, bin_, bri, bii, ctr, cti, m_ref, q_ref, p_ref, a16r_ref, a16i_ref):
    n, t16 = S5_STATE, S5_T
    gw = S5_T * S5_CH
    lane_t = lax.broadcasted_iota(jnp.int32, (n, gw), 1) // S5_CH
    sel = (lax.broadcasted_iota(jnp.int32, (S5_CH, gw), 1) % S5_CH
           == lax.broadcasted_iota(jnp.int32, (S5_CH, gw), 0)).astype(F32)
    lane_blk = lax.broadcasted_iota(jnp.int32, (S5_CH, gw), 1) // S5_CH
    lane128 = lax.broadcasted_iota(jnp.int32, (S5_CH, 2 * n), 1)
    m_ref[...] = jnp.zeros_like(m_ref)
    q_ref[...] = jnp.zeros_like(q_ref)
    p_ref[...] = jnp.zeros_like(p_ref)

    def discretise(a_re, a_im, step):
        a_re = jnp.minimum(a_re, -1e-4)
        mag = jnp.exp(a_re * step)
        ab_re, ab_im = mag * jnp.cos(a_im * step), mag * jnp.sin(a_im * step)
        den = a_re * a_re + a_im * a_im
        nr, ni = ab_re - 1.0, ab_im
        return ab_re, ab_im, (nr * a_re + ni * a_im) / den, (ni * a_re - nr * a_im) / den

    def powers(ab_re, ab_im):
        pr, pi = [jnp.ones_like(ab_re)], [jnp.zeros_like(ab_re)]
        for _ in range(t16):
            pr.append(pr[-1] * ab_re - pi[-1] * ab_im)
            pi.append(pr[-2] * ab_im + pi[-1] * ab_re)
        return pr, pi

    for d in range(2):
        step_r = jnp.exp(jnp.concatenate([jnp.broadcast_to(lst[d, gl], (1, n)) for gl in range(2)], axis=1))
        ab_re, ab_im, cf_re, cf_im = discretise(arr[d], air[d], step_r)
        prr, pir = powers(ab_re, ab_im)
        a16r_ref[d] = prr[t16]
        a16i_ref[d] = pir[t16]
        bt_re = cf_re * bri[...] - cf_im * bii[...]
        bt_im = cf_re * bii[...] + cf_im * bri[...]
        for s in range(t16):
            e = t16 - 1 - s if d == 0 else s
            qre = bt_re * prr[e] - bt_im * pir[e]
            qim = bt_re * pir[e] + bt_im * prr[e]
            for gl in range(2):
                keep = (lane128 // n) == gl
                r0 = gl * gw + s * S5_CH
                q_ref[d, r0:r0 + S5_CH, 0:2 * n] = jnp.where(keep, qre, 0.0).astype(q_ref.dtype)
                q_ref[d, r0:r0 + S5_CH, 2 * n:4 * n] = jnp.where(keep, qim, 0.0).astype(q_ref.dtype)
        for gl in range(2):
            ab_re, ab_im, cf_re, cf_im = discretise(arc[d, gl], aic[d, gl], jnp.exp(lst[d, gl]))
            prc, pic = powers(ab_re, ab_im)
            bb_re = cf_re * brn[gl] - cf_im * bin_[gl]
            bb_im = cf_re * bin_[gl] + cf_im * brn[gl]
            ct_re = _mm_hi(ctr[gl], sel)
            ct_im = _mm_hi(cti[gl], sel)

            def response(exps):
                p_re = jnp.zeros((n, gw), F32)
                p_im = jnp.zeros((n, gw), F32)
                for t in range(t16):
                    p_re = jnp.where(lane_t == t, prc[exps[t]], p_re)
                    p_im = jnp.where(lane_t == t, pic[exps[t]], p_im)
                return ct_re * p_re - ct_im * p_im, ct_re * p_im + ct_im * p_re

            e_re, e_im = response([t if d == 0 else t16 - 1 - t for t in range(t16)])
            r0v = _mm_tn_hi(bb_re, e_re) - _mm_tn_hi(bb_im, e_im)
            for s in range(t16):
                if d == 0:
                    blk = jnp.where(lane_blk >= s, pltpu.roll(r0v, (S5_CH * s) % gw, 1), 0.0)
                else:
                    blk = jnp.where(lane_blk <= s, pltpu.roll(r0v, (gw - S5_CH * (t16 - 1 - s)) % gw, 1), 0.0)
                r0 = gl * gw + s * S5_CH
                m_ref[d, r0:r0 + S5_CH, gl * gw:(gl + 1) * gw] = blk.astype(m_ref.dtype)
            c_re, c_im = response([t + 1 if d == 0 else t16 - t for t in range(t16)])
            p_ref[d, gl * n:(gl + 1) * n, gl * gw:(gl + 1) * gw] = c_re.astype(p_ref.dtype)
            p_ref[d, 2 * n + gl * n:2 * n + (gl + 1) * n, gl * gw:(gl + 1) * gw] = (-c_im).astype(p_ref.dtype)


def _s5_prep(a_re, a_im, log_step, b_re, b_im, c_re, c_im):
    n, g, ch, pr = S5_STATE, S5_GROUPS, S5_CH, S5_PAIRS
    f = lambda t: t.astype(F32)
    arc, aic = f(a_re)[..., None], f(a_im)[..., None]
    arr, air = f(a_re).reshape(2, pr, 1, 2 * n), f(a_im).reshape(2, pr, 1, 2 * n)
    lst = f(log_step).reshape(2, g, 1, 1)
    pair_rows = lambda t: f(t).reshape(pr, 2, n, ch).transpose(0, 3, 1, 2).reshape(pr, ch, 2 * n)
    ctr, cti = f(c_re).transpose(0, 2, 1), f(c_im).transpose(0, 2, 1)
    gw2 = 2 * S5_T * S5_CH
    spec_c = pl.BlockSpec((2, 2, n, 1), lambda p: (0, p, 0, 0))
    spec_r = pl.BlockSpec((2, None, 1, 2 * n), lambda p: (0, p, 0, 0))
    spec_g = pl.BlockSpec((2, n, ch), lambda p: (p, 0, 0))
    spec_p = pl.BlockSpec((None, ch, 2 * n), lambda p: (p, 0, 0))
    outs = pl.pallas_call(
        _s5_prep_kernel, grid=(pr,),
        in_specs=[spec_c, spec_c, spec_r, spec_r, pl.BlockSpec((2, 2, 1, 1), lambda p: (0, p, 0, 0)),
                  spec_g, spec_g, spec_p, spec_p, spec_g, spec_g],
        out_specs=[pl.BlockSpec((2, None, gw2, gw2), lambda p: (0, p, 0, 0)),
                   pl.BlockSpec((2, None, gw2, 4 * n), lambda p: (0, p, 0, 0)),
                   pl.BlockSpec((2, None, 4 * n, gw2), lambda p: (0, p, 0, 0)),
                   pl.BlockSpec((2, None, 1, 2 * n), lambda p: (0, p, 0, 0)),
                   pl.BlockSpec((2, None, 1, 2 * n), lambda p: (0, p, 0, 0))],
        out_shape=[jax.ShapeDtypeStruct((2, pr, gw2, gw2), BF16), jax.ShapeDtypeStruct((2, pr, gw2, 4 * n), BF16),
                   jax.ShapeDtypeStruct((2, pr, 4 * n, gw2), BF16),
                   jax.ShapeDtypeStruct((2, pr, 1, 2 * n), F32), jax.ShapeDtypeStruct((2, pr, 1, 2 * n), F32)],
        compiler_params=_params(("arbitrary",)), name="s5_prep",
    )(arc, aic, arr, air, lst, f(b_re), f(b_im), pair_rows(b_re), pair_rows(b_im), ctr, cti)
    m, q, p, a16r, a16i = outs
    return m, q, p, a16r.reshape(2, 1, g * n), a16i.reshape(2, 1, g * n)


def _s5_state_kernel(u_ref, q_ref, vr_ref, vi_ref):
    v = jnp.dot(u_ref[...], q_ref[...], preferred_element_type=F32)
    vr_ref[...] = v[:, 0:128]
    vi_ref[...] = v[:, 128:256]


def _s5_scan_kernel(vr_ref, vi_ref, ar_ref, ai_ref, h0r_ref, h0i_ref, hr_ref, hi_ref, fr_ref, fi_ref, *, nch, bsz):
    d = pl.program_id(0)
    ar = jnp.broadcast_to(ar_ref[...], h0r_ref.shape)
    ai = jnp.broadcast_to(ai_ref[...], h0r_ref.shape)
    cpt = 8 // bsz
    ntile = nch // cpt

    def run(reverse):
        def body(kk, carry):
            hr, hi = carry
            t = ntile - 1 - kk if reverse else kk
            rows = pl.ds(pl.multiple_of(t * 8, 8), 8)
            vr8, vi8 = vr_ref[rows, :], vi_ref[rows, :]
            in_r, in_i = [None] * cpt, [None] * cpt
            for j in (range(cpt - 1, -1, -1) if reverse else range(cpt)):
                in_r[j], in_i[j] = hr, hi
                sl = slice(j * bsz, (j + 1) * bsz)
                hr, hi = ar * hr - ai * hi + vr8[sl], ar * hi + ai * hr + vi8[sl]
            hr_ref[rows, :] = jnp.concatenate(in_r, axis=0)
            hi_ref[rows, :] = jnp.concatenate(in_i, axis=0)
            return hr, hi

        hr, hi = lax.fori_loop(0, ntile, body, (h0r_ref[...], h0i_ref[...]))
        fr_ref[...] = hr
        fi_ref[...] = hi

    @pl.when(d == 0)
    def _():
        run(False)

    @pl.when(d == 1)
    def _():
        run(True)


def _s5_out_kernel(u_ref, m_ref, p_ref, hr_ref, hi_ref, y_ref):
    u = u_ref[...]
    y = None
    for d in range(2):
        hcat = jnp.concatenate([hr_ref[d], hi_ref[d]], axis=-1).astype(BF16)
        t = jnp.dot(u, m_ref[d], preferred_element_type=F32) + jnp.dot(hcat, p_ref[d], preferred_element_type=F32)
        y = t if y is None else y + t
    y_ref[...] = y


def _s5(ug, mats, h0r, h0i, bsz):
    m, q, p, a16r, a16i = mats
    rows = ug.shape[0]
    nch = rows // bsz
    gn = S5_GROUPS * S5_STATE
    gw2 = 2 * S5_T * S5_CH
    vr, vi = pl.pallas_call(
        _s5_state_kernel, grid=(2, S5_PAIRS),
        in_specs=[pl.BlockSpec((rows, gw2), lambda d, pp: (0, pp)),
                  pl.BlockSpec((None, None, gw2, 256), lambda d, pp: (d, pp, 0, 0))],
        out_specs=[pl.BlockSpec((None, rows, 128), lambda d, pp: (d, 0, pp))] * 2,
        out_shape=[jax.ShapeDtypeStruct((2, rows, gn), F32)] * 2,
        compiler_params=_params(("arbitrary", "arbitrary")), name="s5_state",
    )(ug, q)
    lb = 256
    blk = pl.BlockSpec((None, rows, lb), lambda d, j: (d, 0, j))
    vec = pl.BlockSpec((None, 1, lb), lambda d, j: (d, 0, j))
    st = pl.BlockSpec((None, bsz, lb), lambda d, j: (d, 0, j))
    hr, hi, fr, fi = pl.pallas_call(
        functools.partial(_s5_scan_kernel, nch=nch, bsz=bsz), grid=(2, gn // lb),
        in_specs=[blk, blk, vec, vec, st, st], out_specs=[blk, blk, st, st],
        out_shape=[jax.ShapeDtypeStruct((2, rows, gn), F32)] * 2 + [jax.ShapeDtypeStruct((2, bsz, gn), F32)] * 2,
        compiler_params=_params(("arbitrary", "arbitrary")), name="s5_scan",
    )(vr, vi, a16r, a16i, h0r, h0i)
    y = pl.pallas_call(
        _s5_out_kernel, grid=(S5_PAIRS,),
        in_specs=[pl.BlockSpec((rows, gw2), lambda pp: (0, pp)),
                  pl.BlockSpec((2, None, gw2, gw2), lambda pp: (0, pp, 0, 0)),
                  pl.BlockSpec((2, None, 256, gw2), lambda pp: (0, pp, 0, 0)),
                  pl.BlockSpec((2, rows, 128), lambda pp: (0, 0, pp)),
                  pl.BlockSpec((2, rows, 128), lambda pp: (0, 0, pp))],
        out_specs=pl.BlockSpec((rows, gw2), lambda pp: (0, pp)),
        out_shape=jax.ShapeDtypeStruct((rows, S5_GROUPS * S5_T * S5_CH), F32),
        compiler_params=_params(("arbitrary",)), name="s5_out",
    )(ug, m, p, hr, hi)
    return y, fr, fi


def _s5_to_chunks(u, lead):
    bsz = u.shape[0]
    nch = u.shape[1] // S5_T
    t = u.reshape(bsz, nch, S5_T, S5_GROUPS, S5_CH).transpose(1, 0, 3, 2, 4)
    return t.reshape(nch * bsz, S5_GROUPS * S5_T * S5_CH).astype(BF16)


def _s5_from_chunks(y, bsz):
    nch = y.shape[0] // bsz
    t = y.reshape(nch, bsz, S5_GROUPS, S5_T, S5_CH).transpose(1, 0, 3, 2, 4)
    return t.reshape(bsz, nch * S5_T, S5_GROUPS * S5_CH)


def _post_kernel(x_ref, gt_ref, go_ref, gr_ref, sy_ref, sx_ref, sz_ref, mh_ref, mo_ref, y5_ref, u5_ref,
                 gg_ref, sd_ref, sg_ref, mg_ref, d5_ref, wg_ref, bg_ref, avg_ref, wo_ref, o_ref, *, colmajor, rt):
    def rm(v):
        if not colmajor:
            return v
        return jnp.swapaxes(v, 0, 1).reshape(rt * GRID_W, v.shape[-1])

    def head_rms(t):
        return t * lax.rsqrt(_mm_hi(t * t, avg_ref[...]) + EPS)

    a = head_rms(go_ref[0] + go_ref[1]) * gg_ref[...] * jax.nn.silu(gr_ref[...])
    ys = rm(sy_ref[0] + sy_ref[1] + sd_ref[...] * sx_ref[...])
    b = _rms(ys * jax.nn.silu(rm(sz_ref[...]))) * sg_ref[...]
    m = head_rms(mh_ref[0] + mh_ref[1]) * mg_ref[...] * jax.nn.sigmoid(mo_ref[...])
    y5 = rm(y5_ref[...] + d5_ref[...] * u5_ref[...])
    gl = jax.nn.gelu(y5)
    dd = gl * jax.nn.sigmoid(_mm(gl, wg_ref[...]) + bg_ref[...])
    cat = jnp.concatenate([a, b, m, dd], axis=-1).astype(BF16)
    o_ref[...] = x_ref[...] + gt_ref[...] * jnp.dot(cat, wo_ref[...], preferred_element_type=F32)


def _post(x, modv, gla_o, gla_proj, ssd_y, ssd_xbc, ssd_z, ml_h, ml_vo, s5_y, s5_u, pw, colmajor):
    bsz, length, d = x.shape
    tm = min(512, length)
    rt = tm // GRID_W
    rows = length // GRID_W
    per_batch = modv.shape[0] > 1
    w = GROUP_W

    def rmspec(colblock, dirs=False):
        if dirs:
            return pl.BlockSpec((2, None, tm, w), lambda b, i: (0, b, i, colblock))
        return pl.BlockSpec((None, tm, w), lambda b, i: (b, i, colblock))

    def cmspec(colblock, dirs=False):
        if not colmajor:
            return rmspec(colblock, dirs)
        if dirs:
            return pl.BlockSpec((2, None, GRID_W, rt, w), lambda b, i: (0, b, 0, i, colblock))
        return pl.BlockSpec((None, GRID_W, rt, w), lambda b, i: (b, 0, i, colblock))

    def cm(t):
        return t.reshape(t.shape[:-2] + (GRID_W, rows, t.shape[-1])) if colmajor else t

    return pl.pallas_call(
        functools.partial(_post_kernel, colmajor=colmajor, rt=rt), grid=(bsz, length // tm),
        in_specs=[pl.BlockSpec((None, tm, d), lambda b, i: (b, i, 0)), _mod_spec(5, per_batch),
                  rmspec(0, True), rmspec(3), cmspec(0, True), cmspec(0), cmspec(0),
                  rmspec(0, True), rmspec(1), cmspec(0), cmspec(0),
                  _const((1, w)), _const((1, w)), _const((1, w)), _const((1, w)), _const((1, w)),
                  _const((w, w)), _const((1, w)), _const((w, w)), _const((4 * w, d))],
        out_specs=pl.BlockSpec((None, tm, d), lambda b, i: (b, i, 0)),
        out_shape=jax.ShapeDtypeStruct(x.shape, F32),
        compiler_params=_params(("arbitrary", "arbitrary")), name="post",
    )(x, modv, gla_o, gla_proj, cm(ssd_y), cm(ssd_xbc), cm(ssd_z), ml_h, ml_vo, cm(s5_y), cm(s5_u),
      pw["gla_g"], pw["ssd_d"], pw["ssd_g"], pw["ml_g"], pw["s5_d"], pw["w_glu"], pw["b_glu"], pw["avg"],
      pw["w_out"])


def _token_mixing(h_in, modv, g1, lw, states, colmajor):
    bsz, length, _ = h_in.shape
    pr = _inproj(h_in, modv, g1, lw["w_in"], colmajor)
    xbc = _conv_silu(pr["ssd_xbc"], lw["ssd_conv_w"], lw["ssd_conv_b"])
    mqk = _conv_silu(pr["ml_qk"], lw["ml_conv_w"], lw["ml_conv_b"])
    gla_o, gla_s = _gla(pr["gla"], pr["small"], lw["gla_w_lr2"], lw["gla_b_lr2"], states["gla"])
    ssd_y, ssd_s = _ssd(xbc, pr["small_cm"], lw["ssd_a_log"], lw["ssd_dt_bias"], states["ssd"])
    ml_h, ml_c, ml_nm = _mlstm(mqk, pr["ml_vo"], pr["small"], lw["ml_gate_bias"], states["ml_c"], states["ml_nm"])
    y5, s5r, s5i = _s5(_s5_to_chunks(pr["s5"], None), lw["s5_mats"], states["s5_r"], states["s5_i"], bsz)
    outs = dict(gla_o=gla_o, gla_proj=pr["gla"], ssd_y=ssd_y, ssd_xbc=xbc, ssd_z=pr["ssd_z"], ml_h=ml_h,
                ml_vo=pr["ml_vo"], s5_y=_s5_from_chunks(y5, bsz), s5_u=pr["s5"])
    finals = dict(gla=gla_s, ssd=ssd_s, ml_c=ml_c, ml_nm=ml_nm, s5_r=s5r, s5_i=s5i)
    return outs, finals


def _proj_weight(w):
    cols = [w[:, 0:1024], w[:, 1056:1568], w[:, 1568:1824], w[:, 1832:2344], w[:, 2344:2856], w[:, 2872:3128],
            w[:, 1024:1056], w[:, 1824:1832], w[:, 2856:2872], jnp.zeros((w.shape[0], SMALL_W - 56), w.dtype)]
    return jnp.concatenate(cols, axis=1).astype(BF16)


def kernel(x, c, ctx, c_ctx, w_mod, b_mod, g_norm, ffn_w_in, ffn_w_out, w_in, w_out, gla_w_lr2, gla_b_lr2,
           gla_g_norm, ssd_conv_w, ssd_conv_b, ssd_a_log, ssd_dt_bias, ssd_d, ssd_g_norm, ml_conv_w, ml_conv_b,
           ml_gate_bias, ml_g_norm, s5_a_re, s5_a_im, s5_log_step, s5_b_re, s5_b_im, s5_c_re, s5_c_im, s5_d,
           s5_w_glu, s5_b_glu, g_final):
    bsz, length, d = x.shape
    depth = w_mod.shape[0]
    assert bsz + 1 <= MOD_ROWS and d == D_MODEL and length % 1024 == 0 and ctx.shape[1] % SCAN_TB == 0
    cc = jnp.concatenate([c, c_ctx[None, :], jnp.zeros((MOD_ROWS - bsz - 1, d), F32)], axis=0)
    mod = _modulation(cc, w_mod, b_mod)
    avg = jnp.kron(jnp.eye(N_HEADS, dtype=F32), jnp.full((HEAD_DIM, HEAD_DIM), 1.0 / HEAD_DIM, F32))
    row = lambda t: t.reshape(1, -1).astype(F32)
    zero_states = dict(
        gla=jnp.zeros((bsz, 2, HW, HW), F32), ssd=jnp.zeros((bsz, 2, HW, HW), F32),
        ml_c=jnp.zeros((bsz, 2, HW, HW), F32), ml_nm=jnp.zeros((bsz, 2, 8, HW), F32),
        s5_r=jnp.zeros((2, bsz, S5_GROUPS * S5_STATE), F32), s5_i=jnp.zeros((2, bsz, S5_GROUPS * S5_STATE), F32))
    for l in range(depth):
        last = l == depth - 1
        mod_x = mod[l, :bsz].reshape(bsz, N_MOD, 1, d)
        mod_c = mod[l, bsz:bsz + 1].reshape(1, N_MOD, 1, d)
        wi = [ffn_w_in[l, j].astype(BF16) for j in range(2)]
        wo = [ffn_w_out[l, j].astype(BF16) for j in range(2)]
        g = [row(g_norm[l, j]) for j in range(3)]
        lw = dict(
            w_in=_proj_weight(w_in[l]), gla_w_lr2=gla_w_lr2[l], gla_b_lr2=gla_b_lr2[l][:, None, :],
            ssd_conv_w=ssd_conv_w[l], ssd_conv_b=row(ssd_conv_b[l]), ssd_a_log=ssd_a_log[l].astype(F32),
            ssd_dt_bias=ssd_dt_bias[l], ml_conv_w=ml_conv_w[l], ml_conv_b=row(ml_conv_b[l]),
            ml_gate_bias=ml_gate_bias[l],
            s5_mats=_s5_prep(s5_a_re[l], s5_a_im[l], s5_log_step[l], s5_b_re[l], s5_b_im[l], s5_c_re[l], s5_c_im[l]))
        pw = dict(gla_g=row(gla_g_norm[l]), ssd_d=row(jnp.repeat(ssd_d[l], HEAD_DIM)), ssd_g=row(ssd_g_norm[l]),
                  ml_g=row(ml_g_norm[l]), s5_d=row(s5_d[l]), w_glu=s5_w_glu[l].astype(BF16), b_glu=row(s5_b_glu[l]),
                  avg=avg, w_out=w_out[l].astype(BF16))
        x = _ffn(x, mod_x, 0, g[0], wi[0], wo[0])
        ctx = _ffn(ctx, mod_c, 0, g[0], wi[0], wo[0])
        outs_c, finals = _token_mixing(ctx, mod_c, g[1], lw, zero_states, colmajor=False)
        outs_x, _ = _token_mixing(x, mod_x, g[1], lw, finals, colmajor=True)
        x = _post(x, mod_x, pw=pw, colmajor=True, **outs_x)
        x = _ffn(x, mod_x, 2, g[2], wi[1], wo[1], g_final=row(g_final) if last else None)
        if not last:
            ctx = _post(ctx, mod_c, pw=pw, colmajor=False, **outs_c)
            ctx = _ffn(ctx, mod_c, 2, g[2], wi[1], wo[1])
    return x
```

```python
import functools

import jax
import jax.numpy as jnp
from jax import lax
from jax.experimental import pallas as pl
from jax.experimental.pallas import tpu as pltpu

F32 = jnp.float32
BF16 = jnp.bfloat16
HI = lax.Precision.HIGHEST

D_MODEL = 1024
D_FF = 2816
GRID_W = 64
GROUP_W = 256
HEAD_DIM = 64
N_HEADS = 4
CHUNK = 64
EPS = 1e-6
N_MOD = 9
GLA_RANK = 16
GLA_NORMALISER = 16.0
S5_CH = 16
S5_GROUPS = 16
S5_STATE = 64
S5_T = 16
S5_PAIRS = S5_GROUPS // 2
SMALL_W = 128
PROJ_W = 3072 + SMALL_W
MOD_ROWS = 8
VMEM_LIMIT = 56 * 1024 * 1024


def _params(sem, vmem=VMEM_LIMIT):
    return pltpu.CompilerParams(dimension_semantics=sem, vmem_limit_bytes=vmem)


def _mm(a, b):
    return jnp.dot(a.astype(BF16), b.astype(BF16), preferred_element_type=F32)


def _mm_nt(a, b):
    return lax.dot_general(a.astype(BF16), b.astype(BF16), (((1,), (1,)), ((), ())), preferred_element_type=F32)


def _mm_tn(a, b):
    return lax.dot_general(a.astype(BF16), b.astype(BF16), (((0,), (0,)), ((), ())), preferred_element_type=F32)


def _mm_hi(a, b):
    return jnp.dot(a, b, precision=HI, preferred_element_type=F32)


def _mm_tn_hi(a, b):
    return lax.dot_general(a, b, (((0,), (0,)), ((), ())), precision=HI, preferred_element_type=F32)


def _rms(t):
    return t * lax.rsqrt(jnp.mean(t * t, axis=-1, keepdims=True) + EPS)


def _modnorm(xv, g, shift, scale):
    return (_rms(xv) * g) * (1.0 + scale) + shift


def _const(shape):
    n = len(shape)
    return pl.BlockSpec(shape, lambda *_: (0,) * n, pipeline_mode=pl.Buffered(1))


def _mod_kernel(c_ref, w_ref, b_ref, o_ref):
    o_ref[...] = _mm_hi(jax.nn.silu(c_ref[...]), w_ref[...]) + b_ref[...]


def _modulation(cc, w_mod, b_mod):
    depth, d, n = w_mod.shape
    tn = 1024
    return pl.pallas_call(
        _mod_kernel, grid=(depth, n // tn),
        in_specs=[pl.BlockSpec((MOD_ROWS, d), lambda l, j: (0, 0)),
                  pl.BlockSpec((None, d, tn), lambda l, j: (l, 0, j)),
                  pl.BlockSpec((None, 1, tn), lambda l, j: (l, 0, j))],
        out_specs=pl.BlockSpec((None, MOD_ROWS, tn), lambda l, j: (l, 0, j)),
        out_shape=jax.ShapeDtypeStruct((depth, MOD_ROWS, n), F32),
        compiler_params=_params(("arbitrary", "arbitrary")), name="modulation",
    )(cc, w_mod, b_mod.reshape(depth, 1, n))


def _mod_spec(k, per_batch):
    if per_batch:
        return pl.BlockSpec((None, None, 1, D_MODEL), lambda b, i: (b, k, 0, 0))
    return pl.BlockSpec((None, None, 1, D_MODEL), lambda b, i: (0, k, 0, 0))


def _ffn_kernel(x_ref, sh_ref, sc_ref, gt_ref, g_ref, wi_ref, wo_ref, *rest, final):
    o_ref = rest[-1]
    xv = x_ref[...]
    h = _modnorm(xv, g_ref[...], sh_ref[...], sc_ref[...]).astype(BF16)
    gu = jnp.dot(h, wi_ref[...], preferred_element_type=F32)
    a = (jax.nn.silu(gu[:, :D_FF]) * gu[:, D_FF:]).astype(BF16)
    y = jnp.dot(a, wo_ref[...], preferred_element_type=F32)
    out = xv + (0.5 * gt_ref[...]) * y
    if final:
        out = _rms(out) * rest[0][...]
    o_ref[...] = out


def _ffn(x, modv, j, g, w_in, w_out, g_final=None):
    bsz, length, d = x.shape
    tm = min(512, length)
    per_batch = modv.shape[0] > 1
    final = g_final is not None
    ins = [x, modv, modv, modv, g, w_in, w_out]
    specs = [pl.BlockSpec((None, tm, d), lambda b, i: (b, i, 0)),
             _mod_spec(3 * j, per_batch), _mod_spec(3 * j + 1, per_batch), _mod_spec(3 * j + 2, per_batch),
             _const((1, d)), _const((d, 2 * D_FF)), _const((D_FF, d))]
    if final:
        ins.append(g_final)
        specs.append(_const((1, d)))
    return pl.pallas_call(
        functools.partial(_ffn_kernel, final=final), grid=(bsz, length // tm),
        in_specs=specs, out_specs=pl.BlockSpec((None, tm, d), lambda b, i: (b, i, 0)),
        out_shape=jax.ShapeDtypeStruct(x.shape, F32),
        compiler_params=_params(("arbitrary", "arbitrary")), name="ffn",
    )(*ins)


_PROJ_SPLIT = (("gla", 0, 1024, False), ("ml_qk", 1792, 2304, False), ("ml_vo", 2304, 2816, False),
               ("small", 3072, 3200, False), ("ssd_xbc", 1024, 1536, True), ("ssd_z", 1536, 1792, True),
               ("s5", 2816, 3072, False), ("small_cm", 3072, 3200, True))
S5_HALF = S5_T // 2
S5_HALF_W = S5_PAIRS * S5_HALF * 2 * S5_CH


def _to_s5_chunks(ys):
    yt = ys.T
    pw = 2 * S5_CH
    outs = []
    for pp in range(S5_PAIRS):
        rows = yt[pp * pw:(pp + 1) * pw]
        z = jnp.concatenate([rows[:, s2 * 128:(s2 + 1) * 128] for s2 in range(S5_HALF // 2)], axis=0)
        zt = z.T
        outs.append(jnp.concatenate([zt[0:GRID_W], zt[GRID_W:2 * GRID_W]], axis=1))
    return jnp.concatenate(outs, axis=1)


def _from_s5_chunks(yc):
    pw = 2 * S5_CH
    cw = S5_HALF * pw
    rows = []
    for pp in range(S5_PAIRS):
        blk = yc[:, pp * cw:(pp + 1) * cw]
        z = jnp.concatenate([blk[:, 0:cw // 2], blk[:, cw // 2:cw]], axis=0).T
        rows.append(jnp.concatenate([z[s2 * pw:(s2 + 1) * pw] for s2 in range(S5_HALF // 2)], axis=1))
    return jnp.concatenate(rows, axis=0).T


def _inproj_kernel(x_ref, sh_ref, sc_ref, g_ref, w_ref, *outs, colmajor, rt):
    h = _modnorm(x_ref[...], g_ref[...], sh_ref[...], sc_ref[...]).astype(BF16)
    y = jnp.dot(h, w_ref[...], preferred_element_type=F32)
    for o_ref, (name, lo, hi, cm) in zip(outs, _PROJ_SPLIT):
        v = y[:, lo:hi]
        if cm and colmajor:
            v = jnp.swapaxes(v.reshape(rt, GRID_W, hi - lo), 0, 1)
        o_ref[...] = v
        if name == "s5" and colmajor:
            outs[-1][...] = _to_s5_chunks(v)


def _inproj(x, modv, g, w, colmajor):
    bsz, length, d = x.shape
    tm = min(512, length)
    rt = tm // GRID_W
    rows = length // GRID_W
    per_batch = modv.shape[0] > 1
    shapes, specs = [], []
    for _, lo, hi, cm in _PROJ_SPLIT:
        n = hi - lo
        if cm and colmajor:
            shapes.append(jax.ShapeDtypeStruct((bsz, GRID_W, rows, n), F32))
            specs.append(pl.BlockSpec((None, GRID_W, rt, n), lambda b, i: (b, 0, i, 0)))
        else:
            shapes.append(jax.ShapeDtypeStruct((bsz, length, n), F32))
            specs.append(pl.BlockSpec((None, tm, n), lambda b, i: (b, i, 0)))
    if colmajor:
        assert rt == S5_HALF and rows % S5_T == 0
        shapes.append(jax.ShapeDtypeStruct((bsz, rows // S5_T, GRID_W, 2 * S5_HALF_W), F32))
        specs.append(pl.BlockSpec((None, None, GRID_W, S5_HALF_W), lambda b, i: (b, i // 2, 0, i % 2)))
    outs = pl.pallas_call(
        functools.partial(_inproj_kernel, colmajor=colmajor, rt=rt), grid=(bsz, length // tm),
        in_specs=[pl.BlockSpec((None, tm, d), lambda b, i: (b, i, 0)),
                  _mod_spec(3, per_batch), _mod_spec(4, per_batch), _const((1, d)), _const((d, PROJ_W))],
        out_specs=specs, out_shape=shapes,
        compiler_params=_params(("arbitrary", "arbitrary")), name="inproj",
    )(x, modv, modv, g, w)
    res = {}
    for o, (name, lo, hi, cm) in zip(outs, _PROJ_SPLIT):
        res[name] = o.reshape(bsz, length, hi - lo)
    if colmajor:
        res["s5_chunks"] = outs[-1]
    return res


def _conv_kernel(x_ref, p_ref, n_ref, w_ref, b_ref, o_ref, *, tb):
    i = pl.program_id(1)
    nb = pl.num_programs(1)
    x = x_ref[...]
    rid = lax.broadcasted_iota(jnp.int32, x.shape, 0)
    prev = jnp.where(i > 0, p_ref[7:8, :], 0.0)
    nxt = jnp.where(i < nb - 1, n_ref[0:1, :], 0.0)
    xm = jnp.where(rid == 0, prev, pltpu.roll(x, 1, 0))
    xp = jnp.where(rid == tb - 1, nxt, pltpu.roll(x, tb - 1, 0))
    y = w_ref[0:1, :] * xm + w_ref[1:2, :] * x + w_ref[2:3, :] * xp + b_ref[...]
    o_ref[...] = jax.nn.silu(y)


def _conv_silu(x, w, b):
    bsz, length, ch = x.shape
    tb = min(1024, length)
    t8 = tb // 8
    last8 = length // 8 - 1
    return pl.pallas_call(
        functools.partial(_conv_kernel, tb=tb), grid=(bsz, length // tb),
        in_specs=[pl.BlockSpec((None, tb, ch), lambda bb, i: (bb, i, 0)),
                  pl.BlockSpec((None, 8, ch), lambda bb, i: (bb, jnp.maximum(i * t8 - 1, 0), 0)),
                  pl.BlockSpec((None, 8, ch), lambda bb, i: (bb, jnp.minimum((i + 1) * t8, last8), 0)),
                  _const((3, ch)), _const((1, ch))],
        out_specs=pl.BlockSpec((None, tb, ch), lambda bb, i: (bb, i, 0)),
        out_shape=jax.ShapeDtypeStruct(x.shape, F32),
        compiler_params=_params(("arbitrary", "arbitrary")), name="conv_silu",
    )(x, x, x, w, b)


HW = N_HEADS * HEAD_DIM
CPB = 4
SCAN_TB = CPB * CHUNK


def _scan_block(d, i, nb):
    return jnp.where(d == 0, i, nb - 1 - i)


def _chunk_rows(d, cc):
    c = jnp.where(d == 0, cc, CPB - 1 - cc)
    return pl.ds(pl.multiple_of(c * CHUNK, CHUNK), CHUNK)


def _scan_consts(d):
    t = lax.broadcasted_iota(jnp.int32, (CHUNK, HW), 0)
    lane = lax.broadcasted_iota(jnp.int32, (CHUNK, HW), 1)
    s = jnp.bitwise_and(lane, HEAD_DIM - 1)
    vis = jnp.where(d == 0, t - s, s - t)
    r64 = lax.broadcasted_iota(jnp.int32, (CHUNK, CHUNK), 0)
    c64 = lax.broadcasted_iota(jnp.int32, (CHUNK, CHUNK), 1)
    rb = lax.broadcasted_iota(jnp.int32, (HW, HW), 0) // HEAD_DIM
    cb = lax.broadcasted_iota(jnp.int32, (HW, HW), 1) // HEAD_DIM
    return dict(
        mask4=vis >= 0,
        maskt4=(vis <= 0).astype(F32),
        eye4=(vis == 0).astype(F32),
        cumsum=(jnp.where(d == 0, r64 - c64, c64 - r64) >= 0).astype(BF16),
        bd=(rb == cb).astype(BF16),
        head=lane // HEAD_DIM)


def _split(x, terms):
    parts = []
    for _ in range(terms - 1):
        parts.append(x.astype(BF16))
        x = x - parts[-1].astype(F32)
    return parts + [x.astype(BF16)]


def _mm_r01(x, w01, terms=2):
    return sum(jnp.dot(p, w01, preferred_element_type=F32) for p in _split(x, terms))


def _mm_l01(w01, x, terms=3):
    return sum(jnp.dot(w01, p, preferred_element_type=F32) for p in _split(x, terms))


def _bd(x, bd01):
    return jnp.concatenate([x.astype(BF16)] * N_HEADS, axis=0) * bd01


def _scan_specs(nb, bsz):
    def blk(width, colblock):
        return pl.BlockSpec((bsz, SCAN_TB, width), lambda d, i: (0, _scan_block(d, i, nb), colblock))
    out = pl.BlockSpec((None, bsz, SCAN_TB, GROUP_W), lambda d, i: (d, 0, _scan_block(d, i, nb), 0))
    state = pl.BlockSpec((bsz, None, HW, HW), lambda d, i: (0, d, 0, 0))
    return blk, out, state


def _dir_spec(shape):
    n = len(shape)
    return pl.BlockSpec((None,) + shape, lambda d, i: (d,) + (0,) * n)


def _gla_kernel(q_ref, k_ref, v_ref, sm_ref, w_ref, b_ref, s0_ref, o_ref, sf_ref, s_ref, *, bsz):
    d = pl.program_id(0)
    i = pl.program_id(1)

    @pl.when(i == 0)
    def _():
        s_ref[...] = s0_ref[...]

    cs = _scan_consts(d)
    bs = range(bsz)
    for cc in range(CPB):
        rows = _chunk_rows(d, cc)
        sm = sm_ref[:, rows, :].reshape(bsz * CHUNK, SMALL_W)
        lr = jnp.where(d == 0, sm[:, 0:GLA_RANK], sm[:, GLA_RANK:2 * GLA_RANK])
        g = jax.nn.log_sigmoid(_mm(lr, w_ref[...]) + b_ref[...]) * (1.0 / GLA_NORMALISER)
        g = [g[b * CHUNK:(b + 1) * CHUNK] for b in bs]
        bc = _mm_l01(cs["cumsum"], jnp.concatenate(g, axis=1))
        bc = [bc[:, b * HW:(b + 1) * HW] for b in bs]
        bt = [jnp.sum(g[b], axis=0, keepdims=True) for b in bs]
        k = [k_ref[b, rows, :] for b in bs]
        v = [v_ref[b, rows, :].astype(BF16) for b in bs]
        qt = [(q_ref[b, rows, :] * (HEAD_DIM ** -0.5) * jnp.exp(bc[b])).astype(BF16) for b in bs]
        st = [s_ref[b] for b in bs]
        o_st = [_mm_nt(qt[b], st[b].astype(BF16) * cs["bd"]) for b in bs]
        kbd = [_bd(k[b] * jnp.exp(-bc[b]), cs["bd"]) for b in bs]
        att = [jnp.where(cs["mask4"], _mm_nt(qt[b], kbd[b]), 0.0) for b in bs]
        ks = [(k[b] * jnp.exp(bt[b] - bc[b])).astype(BF16) for b in bs]
        upd = [_mm_tn(v[b], ks[b]) for b in bs]
        vbd = [_bd(v[b], cs["bd"]) for b in bs]
        for b in bs:
            o_ref[b, rows, :] = _mm(att[b], vbd[b]) + o_st[b]
        for b in bs:
            s_ref[b] = st[b] * jnp.exp(bt[b]) + upd[b]

    @pl.when(i == pl.num_programs(1) - 1)
    def _():
        sf_ref[...] = s_ref[...]


def _gla(proj, small, w_lr2, b_lr2, s0):
    bsz, length, _ = proj.shape
    nb = length // SCAN_TB
    blk, out, state = _scan_specs(nb, bsz)
    return pl.pallas_call(
        functools.partial(_gla_kernel, bsz=bsz), grid=(2, nb),
        in_specs=[blk(GROUP_W, 0), blk(GROUP_W, 1), blk(GROUP_W, 2), blk(SMALL_W, 0),
                  _dir_spec((GLA_RANK, GROUP_W)), _dir_spec((1, GROUP_W)), state],
        out_specs=[out, state],
        out_shape=[jax.ShapeDtypeStruct((2, bsz, length, GROUP_W), F32), jax.ShapeDtypeStruct(s0.shape, F32)],
        scratch_shapes=[pltpu.VMEM((bsz, HW, HW), F32)],
        compiler_params=_params(("arbitrary",) * 2), name="gla",
    )(proj, proj, proj, small, w_lr2, b_lr2, s0)


def _ssd_kernel(x_ref, b_ref, c_ref, sm_ref, al_ref, db_ref, s0_ref, o_ref, sf_ref, s_ref, *, bsz):
    d = pl.program_id(0)
    i = pl.program_id(1)

    @pl.when(i == 0)
    def _():
        s_ref[...] = s0_ref[...]

    cs = _scan_consts(d)
    r = lax.broadcasted_iota(jnp.int32, (SMALL_W, HW), 0)
    c = lax.broadcasted_iota(jnp.int32, (SMALL_W, HW), 1)
    e_dt = (r == 32 + N_HEADS * d + c // HEAD_DIM).astype(BF16)
    e_grp = ((r // HEAD_DIM == c // (2 * HEAD_DIM)) & (r % HEAD_DIM == c % HEAD_DIM)).astype(BF16)
    rg = lax.broadcasted_iota(jnp.int32, (HW, 2 * HEAD_DIM), 0) // (2 * HEAD_DIM)
    cg = lax.broadcasted_iota(jnp.int32, (HW, 2 * HEAD_DIM), 1) // HEAD_DIM
    grp = (rg == cg).astype(BF16)
    nega = -jnp.exp(al_ref[...])
    bs = range(bsz)
    for cc in range(CPB):
        rows = _chunk_rows(d, cc)
        sm = sm_ref[:, rows, :].reshape(bsz * CHUNK, SMALL_W)
        dt = jax.nn.softplus(_mm_r01(sm, e_dt) + db_ref[...])
        a = dt * nega
        dt = [dt[b * CHUNK:(b + 1) * CHUNK] for b in bs]
        a = [a[b * CHUNK:(b + 1) * CHUNK] for b in bs]
        cum = _mm_l01(cs["cumsum"], jnp.concatenate(a, axis=1))
        cum = [cum[:, b * HW:(b + 1) * HW] for b in bs]
        bm = b_ref[:, rows, :].reshape(bsz * CHUNK, 2 * HEAD_DIM).astype(BF16)
        cm = c_ref[:, rows, :].reshape(bsz * CHUNK, 2 * HEAD_DIM).astype(BF16)
        cmx = _mm(cm, e_grp)
        bmx = _mm(bm, e_grp)
        bm = [bm[b * CHUNK:(b + 1) * CHUNK] for b in bs]
        cm = [cm[b * CHUNK:(b + 1) * CHUNK] for b in bs]
        st = [s_ref[b] for b in bs]
        y_st = [_mm(cmx[b * CHUNK:(b + 1) * CHUNK], st[b].astype(BF16) * cs["bd"]) for b in bs]
        cb4 = [_mm_nt(cm[b], jnp.concatenate([bm[b]] * N_HEADS, axis=0) * grp) for b in bs]
        tot = [jnp.sum(a[b], axis=0, keepdims=True) for b in bs]
        cumr = [jnp.sum(a[b] * cs["maskt4"], axis=0, keepdims=True) for b in bs]
        xdt = [(x_ref[b, rows, :] * dt[b]).astype(BF16) for b in bs]
        bw = [(bmx[b * CHUNK:(b + 1) * CHUNK] * jnp.exp(tot[b] - cum[b])).astype(BF16) for b in bs]
        upd = [_mm_tn(bw[b], xdt[b]) for b in bs]
        xbd = [_bd(xdt[b], cs["bd"]) for b in bs]
        sc = [cb4[b] * jnp.exp(jnp.where(cs["mask4"], cum[b] - cumr[b], -jnp.inf)) for b in bs]
        for b in bs:
            o_ref[b, rows, :] = _mm(sc[b], xbd[b]) + jnp.exp(cum[b]) * y_st[b]
        for b in bs:
            s_ref[b] = st[b] * jnp.exp(tot[b]) + upd[b]

    @pl.when(i == pl.num_programs(1) - 1)
    def _():
        sf_ref[...] = s_ref[...]


def _per_head_lanes(t):
    return jnp.repeat(t.astype(F32), HEAD_DIM, axis=-1)[:, None, :]


def _ssd(xbc, small, a_log, dt_bias, s0):
    bsz, length, _ = xbc.shape
    nb = length // SCAN_TB
    blk, out, state = _scan_specs(nb, bsz)
    return pl.pallas_call(
        functools.partial(_ssd_kernel, bsz=bsz), grid=(2, nb),
        in_specs=[blk(GROUP_W, 0), blk(128, 2), blk(128, 3), blk(SMALL_W, 0),
                  _dir_spec((1, HW)), _dir_spec((1, HW)), state],
        out_specs=[out, state],
        out_shape=[jax.ShapeDtypeStruct((2, bsz, length, GROUP_W), F32), jax.ShapeDtypeStruct(s0.shape, F32)],
        scratch_shapes=[pltpu.VMEM((bsz, HW, HW), F32)],
        compiler_params=_params(("arbitrary",) * 2), name="ssd",
    )(xbc, xbc, xbc, small, _per_head_lanes(a_log), _per_head_lanes(dt_bias), s0)


def _mlstm_kernel(q_ref, k_ref, v_ref, sm_ref, gi_ref, gf_ref, c0_ref, nm0_ref, o_ref, cf_ref, nmf_ref,
                  c_ref, nm_ref, *, bsz):
    d = pl.program_id(0)
    i = pl.program_id(1)

    @pl.when(i == 0)
    def _():
        c_ref[...] = c0_ref[...]
        nm_ref[...] = nm0_ref[...]

    cs = _scan_consts(d)
    r = lax.broadcasted_iota(jnp.int32, (SMALL_W, 2 * HW), 0)
    c = lax.broadcasted_iota(jnp.int32, (SMALL_W, 2 * HW), 1)
    gate = 40 + 2 * N_HEADS * d + N_HEADS * (c // HW) + (c % HW) // HEAD_DIM
    e_if = (r == gate).astype(BF16)
    rid = lax.broadcasted_iota(jnp.int32, (CHUNK, 1), 0)
    last = rid == jnp.where(d == 0, CHUNK - 1, 0)
    bs = range(bsz)

    def per_b(t):
        return [t[b * CHUNK:(b + 1) * CHUNK] for b in bs]

    for cc in range(CPB):
        rows = _chunk_rows(d, cc)
        sm = sm_ref[:, rows, :].reshape(bsz * CHUNK, SMALL_W)
        gates = _mm_r01(sm, e_if)
        gi = per_b(gates[:, :HW] + gi_ref[...])
        lf = per_b(jax.nn.log_sigmoid(gates[:, HW:] + gf_ref[...]))
        fc = _mm_l01(cs["cumsum"], jnp.concatenate(lf, axis=1))
        fc = [fc[:, b * HW:(b + 1) * HW] for b in bs]
        q = [q_ref[b, rows, :] for b in bs]
        qb = [q[b].astype(BF16) for b in bs]
        k = [k_ref[b, rows, :] * (HEAD_DIM ** -0.5) for b in bs]
        v = [v_ref[b, rows, :].astype(BF16) for b in bs]
        cst = [c_ref[b] for b in bs]
        ns = [nm_ref[b, 0:1, :] for b in bs]
        ms = [nm_ref[b, 1:2, :] for b in bs]
        n_st = [_mm(qb[b], cst[b].astype(BF16) * cs["bd"]) for b in bs]
        d_st = [_mm_r01(q[b] * ns[b], cs["bd"]) for b in bs]
        qk = [_mm_nt(qb[b], _bd(k[b], cs["bd"])) for b in bs]
        ftot = [jnp.sum(lf[b], axis=0, keepdims=True) for b in bs]
        fr = [jnp.sum(lf[b] * cs["maskt4"], axis=0, keepdims=True) for b in bs]
        ir = [jnp.sum(gi[b] * cs["eye4"], axis=0, keepdims=True) for b in bs]
        li = [jnp.where(cs["mask4"], fc[b] - fr[b] + ir[b], -jnp.inf) for b in bs]
        m = []
        for b in bs:
            rmax = jnp.zeros_like(li[b])
            for h in range(N_HEADS):
                mh = jnp.max(li[b][:, h * HEAD_DIM:(h + 1) * HEAD_DIM], axis=-1, keepdims=True)
                rmax = jnp.where(cs["head"] == h, mh, rmax)
            m.append(jnp.maximum(fc[b] + ms[b], rmax))
        m_new = [jnp.sum(jnp.where(last, m[b], 0.0), axis=0, keepdims=True) for b in bs]
        kw = [(k[b] * jnp.exp(ftot[b] - fc[b] + gi[b] - m_new[b])) for b in bs]
        upd = [_mm_tn(kw[b], v[b]) for b in bs]
        wf = [qk[b] * jnp.exp(li[b] - m[b]) for b in bs]
        w = [wf[b].astype(BF16) for b in bs]
        w_lo = [(wf[b] - w[b].astype(F32)).astype(BF16) for b in bs]
        vbd = [_bd(v[b], cs["bd"]) for b in bs]
        n_in = [_mm(w[b], vbd[b]) for b in bs]
        d_in = [_mm(w[b], cs["bd"]) + _mm(w_lo[b], cs["bd"]) for b in bs]
        for b in bs:
            winter = jnp.exp(fc[b] + ms[b] - m[b])
            den = d_in[b] + winter * d_st[b]
            o_ref[b, rows, :] = (n_in[b] + winter * n_st[b]) / jnp.maximum(jnp.abs(den), jnp.exp(-m[b]))
        for b in bs:
            decay = jnp.exp(ftot[b] + ms[b] - m_new[b])
            c_ref[b] = cst[b] * decay + upd[b]
            nm_ref[b, 0:1, :] = decay * ns[b] + jnp.sum(kw[b], axis=0, keepdims=True)
            nm_ref[b, 1:2, :] = m_new[b]

    @pl.when(i == pl.num_programs(1) - 1)
    def _():
        cf_ref[...] = c_ref[...]
        nmf_ref[...] = nm_ref[...]


def _mlstm(qk, vo, small, gate_bias, c0, nm0):
    bsz, length, _ = qk.shape
    nb = length // SCAN_TB
    blk, out, state = _scan_specs(nb, bsz)
    nm_spec = pl.BlockSpec((bsz, None, 8, HW), lambda d, i: (0, d, 0, 0))
    return pl.pallas_call(
        functools.partial(_mlstm_kernel, bsz=bsz), grid=(2, nb),
        in_specs=[blk(GROUP_W, 0), blk(GROUP_W, 1), blk(GROUP_W, 0), blk(SMALL_W, 0),
                  _dir_spec((1, HW)), _dir_spec((1, HW)), state, nm_spec],
        out_specs=[out, state, nm_spec],
        out_shape=[jax.ShapeDtypeStruct((2, bsz, length, GROUP_W), F32), jax.ShapeDtypeStruct(c0.shape, F32),
                   jax.ShapeDtypeStruct(nm0.shape, F32)],
        scratch_shapes=[pltpu.VMEM((bsz, HW, HW), F32), pltpu.VMEM((bsz, 8, HW), F32)],
        compiler_params=_params(("arbitrary",) * 2), name="mlstm",
    )(qk, qk, vo, small, _per_head_lanes(gate_bias[:, 0]), _per_head_lanes(gate_bias[:, 1]), c0, nm0)


def _s5_slot(s, gl):
    return (s // S5_HALF) * (S5_HALF * 2 * S5_CH) + (s % 2) * 128 + ((s % S5_HALF) // 2) * 2 * S5_CH + gl * S5_CH


def _s5_prep_kernel(arc, aic, arr, air, lst, brn, bin_, bri, bii, ctr, cti, m_ref, q_ref, p_ref, a16r_ref, a16i_ref):
    n, t16 = S5_STATE, S5_T
    gw = S5_T * S5_CH
    lane_t = lax.broadcasted_iota(jnp.int32, (n, gw), 1) // S5_CH
    sel = (lax.broadcasted_iota(jnp.int32, (S5_CH, gw), 1) % S5_CH
           == lax.broadcasted_iota(jnp.int32, (S5_CH, gw), 0)).astype(F32)
    lane_blk = lax.broadcasted_iota(jnp.int32, (S5_CH, gw), 1) // S5_CH
    lane128 = lax.broadcasted_iota(jnp.int32, (S5_CH, 2 * n), 1)
    m_ref[...] = jnp.zeros_like(m_ref)
    q_ref[...] = jnp.zeros_like(q_ref)
    p_ref[...] = jnp.zeros_like(p_ref)

    def discretise(a_re, a_im, step):
        a_re = jnp.minimum(a_re, -1e-4)
        mag = jnp.exp(a_re * step)
        ab_re, ab_im = mag * jnp.cos(a_im * step), mag * jnp.sin(a_im * step)
        den = a_re * a_re + a_im * a_im
        nr, ni = ab_re - 1.0, ab_im
        return ab_re, ab_im, (nr * a_re + ni * a_im) / den, (ni * a_re - nr * a_im) / den

    def powers(ab_re, ab_im):
        pr, pi = [jnp.ones_like(ab_re)], [jnp.zeros_like(ab_re)]
        for _ in range(t16):
            pr.append(pr[-1] * ab_re - pi[-1] * ab_im)
            pi.append(pr[-2] * ab_im + pi[-1] * ab_re)
        return pr, pi

    for d in range(2):
        step_r = jnp.exp(jnp.concatenate([jnp.broadcast_to(lst[d, gl], (1, n)) for gl in range(2)], axis=1))
        ab_re, ab_im, cf_re, cf_im = discretise(arr[d], air[d], step_r)
        prr, pir = powers(ab_re, ab_im)
        a16r_ref[d] = prr[t16]
        a16i_ref[d] = pir[t16]
        bt_re = cf_re * bri[...] - cf_im * bii[...]
        bt_im = cf_re * bii[...] + cf_im * bri[...]
        for s in range(t16):
            e = t16 - 1 - s if d == 0 else s
            qre = bt_re * prr[e] - bt_im * pir[e]
            qim = bt_re * pir[e] + bt_im * prr[e]
            for gl in range(2):
                keep = (lane128 // n) == gl
                r0 = _s5_slot(s, gl)
                q_ref[d, r0:r0 + S5_CH, 0:2 * n] = jnp.where(keep, qre, 0.0).astype(q_ref.dtype)
                q_ref[d, r0:r0 + S5_CH, 2 * n:4 * n] = jnp.where(keep, qim, 0.0).astype(q_ref.dtype)
        for gl in range(2):
            ab_re, ab_im, cf_re, cf_im = discretise(arc[d, gl], aic[d, gl], jnp.exp(lst[d, gl]))
            prc, pic = powers(ab_re, ab_im)
            bb_re = cf_re * brn[gl] - cf_im * bin_[gl]
            bb_im = cf_re * bin_[gl] + cf_im * brn[gl]
            ct_re = _mm_hi(ctr[gl], sel)
            ct_im = _mm_hi(cti[gl], sel)

            def response(exps):
                p_re = jnp.zeros((n, gw), F32)
                p_im = jnp.zeros((n, gw), F32)
                for t in range(t16):
                    p_re = jnp.where(lane_t == t, prc[exps[t]], p_re)
                    p_im = jnp.where(lane_t == t, pic[exps[t]], p_im)
                return ct_re * p_re - ct_im * p_im, ct_re * p_im + ct_im * p_re

            e_re, e_im = response([t if d == 0 else t16 - 1 - t for t in range(t16)])
            r0v = _mm_tn_hi(bb_re, e_re) - _mm_tn_hi(bb_im, e_im)
            for s in range(t16):
                if d == 0:
                    blk = jnp.where(lane_blk >= s, pltpu.roll(r0v, (S5_CH * s) % gw, 1), 0.0)
                else:
                    blk = jnp.where(lane_blk <= s, pltpu.roll(r0v, (gw - S5_CH * (t16 - 1 - s)) % gw, 1), 0.0)
                r0 = _s5_slot(s, gl)
                m_ref[d, r0:r0 + S5_CH, gl * gw:(gl + 1) * gw] = blk.astype(m_ref.dtype)
            c_re, c_im = response([t + 1 if d == 0 else t16 - t for t in range(t16)])
            p_ref[d, gl * n:(gl + 1) * n, gl * gw:(gl + 1) * gw] = c_re.astype(p_ref.dtype)
            p_ref[d, 2 * n + gl * n:2 * n + (gl + 1) * n, gl * gw:(gl + 1) * gw] = (-c_im).astype(p_ref.dtype)
    c = lax.broadcasted_iota(jnp.int32, (2 * gw, 2 * gw), 1)
    half = S5_HALF * 2 * S5_CH
    t_of_c = S5_HALF * (c // half) + 2 * ((c % 128) // (2 * S5_CH)) + (c % half) // 128
    src = ((c % (2 * S5_CH)) // S5_CH) * gw + t_of_c * S5_CH + c % S5_CH
    perm = (lax.broadcasted_iota(jnp.int32, (2 * gw, 2 * gw), 0) == src).astype(BF16)
    for d in range(2):
        m_ref[d] = jnp.dot(m_ref[d], perm, preferred_element_type=F32).astype(m_ref.dtype)
        p_ref[d] = jnp.dot(p_ref[d], perm, preferred_element_type=F32).astype(p_ref.dtype)


def _s5_prep(a_re, a_im, log_step, b_re, b_im, c_re, c_im):
    n, g, ch, pr = S5_STATE, S5_GROUPS, S5_CH, S5_PAIRS
    f = lambda t: t.astype(F32)
    arc, aic = f(a_re)[..., None], f(a_im)[..., None]
    arr, air = f(a_re).reshape(2, pr, 1, 2 * n), f(a_im).reshape(2, pr, 1, 2 * n)
    lst = f(log_step).reshape(2, g, 1, 1)
    pair_rows = lambda t: f(t).reshape(pr, 2, n, ch).transpose(0, 3, 1, 2).reshape(pr, ch, 2 * n)
    ctr, cti = f(c_re).transpose(0, 2, 1), f(c_im).transpose(0, 2, 1)
    gw2 = 2 * S5_T * S5_CH
    spec_c = pl.BlockSpec((2, 2, n, 1), lambda p: (0, p, 0, 0))
    spec_r = pl.BlockSpec((2, None, 1, 2 * n), lambda p: (0, p, 0, 0))
    spec_g = pl.BlockSpec((2, n, ch), lambda p: (p, 0, 0))
    spec_p = pl.BlockSpec((None, ch, 2 * n), lambda p: (p, 0, 0))
    outs = pl.pallas_call(
        _s5_prep_kernel, grid=(pr,),
        in_specs=[spec_c, spec_c, spec_r, spec_r, pl.BlockSpec((2, 2, 1, 1), lambda p: (0, p, 0, 0)),
                  spec_g, spec_g, spec_p, spec_p, spec_g, spec_g],
        out_specs=[pl.BlockSpec((2, None, gw2, gw2), lambda p: (0, p, 0, 0)),
                   pl.BlockSpec((2, None, gw2, 4 * n), lambda p: (0, p, 0, 0)),
                   pl.BlockSpec((2, None, 4 * n, gw2), lambda p: (0, p, 0, 0)),
                   pl.BlockSpec((2, None, 1, 2 * n), lambda p: (0, p, 0, 0)),
                   pl.BlockSpec((2, None, 1, 2 * n), lambda p: (0, p, 0, 0))],
        out_shape=[jax.ShapeDtypeStruct((2, pr, gw2, gw2), BF16), jax.ShapeDtypeStruct((2, pr, gw2, 4 * n), BF16),
                   jax.ShapeDtypeStruct((2, pr, 4 * n, gw2), BF16),
                   jax.ShapeDtypeStruct((2, pr, 1, 2 * n), F32), jax.ShapeDtypeStruct((2, pr, 1, 2 * n), F32)],
        compiler_params=_params(("arbitrary",)), name="s5_prep",
    )(arc, aic, arr, air, lst, f(b_re), f(b_im), pair_rows(b_re), pair_rows(b_im), ctr, cti)
    m, q, p, a16r, a16i = outs
    return m, q, p, a16r.reshape(2, 1, g * n), a16i.reshape(2, 1, g * n)


def _s5_state_kernel(u0_ref, u1_ref, q_ref, vr_ref, vi_ref):
    half = q_ref.shape[0] // 2
    v = (jnp.dot(u0_ref[...], q_ref[0:half, :], preferred_element_type=F32)
         + jnp.dot(u1_ref[...], q_ref[half:, :], preferred_element_type=F32))
    vr_ref[...] = v[:, 0:128]
    vi_ref[...] = v[:, 128:256]


def _s5_scan_kernel(vr_ref, vi_ref, ar_ref, ai_ref, h0r_ref, h0i_ref, hr_ref, hi_ref, fr_ref, fi_ref, *, nch, bsz):
    d = pl.program_id(0)
    ar = jnp.broadcast_to(ar_ref[...], h0r_ref.shape)
    ai = jnp.broadcast_to(ai_ref[...], h0r_ref.shape)
    cpt = 8 // bsz
    ntile = nch // cpt

    def run(reverse):
        def body(kk, carry):
            hr, hi = carry
            t = ntile - 1 - kk if reverse else kk
            rows = pl.ds(pl.multiple_of(t * 8, 8), 8)
            vr8, vi8 = vr_ref[rows, :], vi_ref[rows, :]
            in_r, in_i = [None] * cpt, [None] * cpt
            for j in (range(cpt - 1, -1, -1) if reverse else range(cpt)):
                in_r[j], in_i[j] = hr, hi
                sl = slice(j * bsz, (j + 1) * bsz)
                hr, hi = ar * hr - ai * hi + vr8[sl], ar * hi + ai * hr + vi8[sl]
            hr_ref[rows, :] = jnp.concatenate(in_r, axis=0)
            hi_ref[rows, :] = jnp.concatenate(in_i, axis=0)
            return hr, hi

        hr, hi = lax.fori_loop(0, ntile, body, (h0r_ref[...], h0i_ref[...]))
        fr_ref[...] = hr
        fi_ref[...] = hi

    @pl.when(d == 0)
    def _():
        run(False)

    @pl.when(d == 1)
    def _():
        run(True)


def _s5_out_kernel(u0_ref, u1_ref, m_ref, p_ref, hr_ref, hi_ref, y_ref):
    u = jnp.concatenate([u0_ref[...], u1_ref[...]], axis=-1)
    y = None
    for d in range(2):
        hcat = jnp.concatenate([hr_ref[d], hi_ref[d]], axis=-1).astype(BF16)
        t = jnp.dot(u, m_ref[d], preferred_element_type=F32) + jnp.dot(hcat, p_ref[d], preferred_element_type=F32)
        y = t if y is None else y + t
    y_ref[...] = y


def _s5(ug, mats, h0r, h0i, bsz):
    m, q, p, a16r, a16i = mats
    rows = ug.shape[0]
    nch = rows // bsz
    gn = S5_GROUPS * S5_STATE
    gw2 = 2 * S5_T * S5_CH
    hw = gw2 // 2
    vr, vi = pl.pallas_call(
        _s5_state_kernel, grid=(2, S5_PAIRS),
        in_specs=[pl.BlockSpec((rows, hw), lambda d, pp: (0, pp)),
                  pl.BlockSpec((rows, hw), lambda d, pp: (0, S5_PAIRS + pp)),
                  pl.BlockSpec((None, None, gw2, 256), lambda d, pp: (d, pp, 0, 0))],
        out_specs=[pl.BlockSpec((None, rows, 128), lambda d, pp: (d, 0, pp))] * 2,
        out_shape=[jax.ShapeDtypeStruct((2, rows, gn), F32)] * 2,
        compiler_params=_params(("arbitrary", "arbitrary")), name="s5_state",
    )(ug, ug, q)
    lb = 256
    blk = pl.BlockSpec((None, rows, lb), lambda d, j: (d, 0, j))
    vec = pl.BlockSpec((None, 1, lb), lambda d, j: (d, 0, j))
    st = pl.BlockSpec((None, bsz, lb), lambda d, j: (d, 0, j))
    hr, hi, fr, fi = pl.pallas_call(
        functools.partial(_s5_scan_kernel, nch=nch, bsz=bsz), grid=(2, gn // lb),
        in_specs=[blk, blk, vec, vec, st, st], out_specs=[blk, blk, st, st],
        out_shape=[jax.ShapeDtypeStruct((2, rows, gn), F32)] * 2 + [jax.ShapeDtypeStruct((2, bsz, gn), F32)] * 2,
        compiler_params=_params(("arbitrary", "arbitrary")), name="s5_scan",
    )(vr, vi, a16r, a16i, h0r, h0i)
    y = pl.pallas_call(
        _s5_out_kernel, grid=(S5_PAIRS, 2),
        in_specs=[pl.BlockSpec((rows, hw), lambda pp, hf: (0, pp)),
                  pl.BlockSpec((rows, hw), lambda pp, hf: (0, S5_PAIRS + pp)),
                  pl.BlockSpec((2, None, gw2, hw), lambda pp, hf: (0, pp, 0, hf)),
                  pl.BlockSpec((2, None, 256, hw), lambda pp, hf: (0, pp, 0, hf)),
                  pl.BlockSpec((2, rows, 128), lambda pp, hf: (0, 0, pp)),
                  pl.BlockSpec((2, rows, 128), lambda pp, hf: (0, 0, pp))],
        out_specs=pl.BlockSpec((rows, hw), lambda pp, hf: (0, hf * S5_PAIRS + pp)),
        out_shape=jax.ShapeDtypeStruct((rows, 2 * S5_HALF_W), F32),
        compiler_params=_params(("arbitrary", "arbitrary")), name="s5_out",
    )(ug, ug, m, p, hr, hi)
    return y, fr, fi


def _s5_rows_from_grid(uc):
    bsz, r, w, lanes = uc.shape
    return uc.transpose(2, 1, 0, 3).reshape(w * r * bsz, lanes).astype(BF16)


def _s5_rows_to_grid(y, bsz):
    r = y.shape[0] // (GRID_W * bsz)
    return y.reshape(GRID_W, r, bsz, y.shape[1]).transpose(2, 1, 0, 3)


def _s5_rows_from_tokens(u):
    bsz, length, _ = u.shape
    nch = length // S5_T
    t = u.reshape(bsz, nch, 2, S5_HALF // 2, 2, S5_PAIRS, 2, S5_CH)
    t = t.transpose(1, 0, 2, 5, 4, 3, 6, 7)
    return t.reshape(nch * bsz, 2 * S5_HALF_W).astype(BF16)


def _s5_rows_to_tokens(y, bsz):
    nch = y.shape[0] // bsz
    t = y.reshape(nch, bsz, 2, S5_PAIRS, 2, S5_HALF // 2, 2, S5_CH)
    t = t.transpose(1, 0, 2, 5, 4, 3, 6, 7)
    return t.reshape(bsz, nch * S5_T, S5_GROUPS * S5_CH)


def _post_kernel(x_ref, gt_ref, go_ref, gr_ref, sy_ref, sx_ref, sz_ref, mh_ref, mo_ref, y5_ref, u5_ref,
                 gg_ref, sd_ref, sg_ref, mg_ref, d5_ref, wg_ref, bg_ref, avg_ref, wo_ref, o_ref, *, colmajor, rt):
    def rm(v):
        if not colmajor:
            return v
        return jnp.swapaxes(v, 0, 1).reshape(rt * GRID_W, v.shape[-1])

    def head_rms(t):
        return t * lax.rsqrt(_mm_r01(t * t, avg_ref[...], terms=3) + EPS)

    a = head_rms(go_ref[0] + go_ref[1]) * gg_ref[...] * jax.nn.silu(gr_ref[...])
    ys = rm(sy_ref[0] + sy_ref[1] + sd_ref[...] * sx_ref[...])
    b = _rms(ys * jax.nn.silu(rm(sz_ref[...]))) * sg_ref[...]
    m = head_rms(mh_ref[0] + mh_ref[1]) * mg_ref[...] * jax.nn.sigmoid(mo_ref[...])
    y5 = (_from_s5_chunks(y5_ref[...]) if colmajor else y5_ref[...]) + d5_ref[...] * u5_ref[...]
    gl = jax.nn.gelu(y5)
    dd = gl * jax.nn.sigmoid(_mm(gl, wg_ref[...]) + bg_ref[...])
    cat = jnp.concatenate([a, b, m, dd], axis=-1).astype(BF16)
    o_ref[...] = x_ref[...] + gt_ref[...] * jnp.dot(cat, wo_ref[...], preferred_element_type=F32)


def _post(x, modv, gla_o, gla_proj, ssd_y, ssd_xbc, ssd_z, ml_h, ml_vo, s5_y, s5_u, pw, colmajor):
    bsz, length, d = x.shape
    tm = min(512, length)
    rt = tm // GRID_W
    rows = length // GRID_W
    per_batch = modv.shape[0] > 1
    w = GROUP_W

    def rmspec(colblock, dirs=False):
        if dirs:
            return pl.BlockSpec((2, None, tm, w), lambda b, i: (0, b, i, colblock))
        return pl.BlockSpec((None, tm, w), lambda b, i: (b, i, colblock))

    def cmspec(colblock, dirs=False):
        if not colmajor:
            return rmspec(colblock, dirs)
        if dirs:
            return pl.BlockSpec((2, None, GRID_W, rt, w), lambda b, i: (0, b, 0, i, colblock))
        return pl.BlockSpec((None, GRID_W, rt, w), lambda b, i: (b, 0, i, colblock))

    def cm(t):
        return t.reshape(t.shape[:-2] + (GRID_W, rows, t.shape[-1])) if colmajor else t

    s5_spec = pl.BlockSpec((None, None, GRID_W, S5_HALF_W), lambda b, i: (b, i // 2, 0, i % 2)) if colmajor else rmspec(0)
    return pl.pallas_call(
        functools.partial(_post_kernel, colmajor=colmajor, rt=rt), grid=(bsz, length // tm),
        in_specs=[pl.BlockSpec((None, tm, d), lambda b, i: (b, i, 0)), _mod_spec(5, per_batch),
                  rmspec(0, True), rmspec(3), cmspec(0, True), cmspec(0), cmspec(0),
                  rmspec(0, True), rmspec(1), s5_spec, rmspec(0),
                  _const((1, w)), _const((1, w)), _const((1, w)), _const((1, w)), _const((1, w)),
                  _const((w, w)), _const((1, w)), _const((w, w)), _const((4 * w, d))],
        out_specs=pl.BlockSpec((None, tm, d), lambda b, i: (b, i, 0)),
        out_shape=jax.ShapeDtypeStruct(x.shape, F32),
        compiler_params=_params(("arbitrary", "arbitrary")), name="post",
    )(x, modv, gla_o, gla_proj, cm(ssd_y), cm(ssd_xbc), cm(ssd_z), ml_h, ml_vo, s5_y, s5_u,
      pw["gla_g"], pw["ssd_d"], pw["ssd_g"], pw["ml_g"], pw["s5_d"], pw["w_glu"], pw["b_glu"], pw["avg"],
      pw["w_out"])


def _token_mixing(h_in, modv, g1, lw, states, colmajor):
    bsz, length, _ = h_in.shape
    pr = _inproj(h_in, modv, g1, lw["w_in"], colmajor)
    xbc = _conv_silu(pr["ssd_xbc"], lw["ssd_conv_w"], lw["ssd_conv_b"])
    mqk = _conv_silu(pr["ml_qk"], lw["ml_conv_w"], lw["ml_conv_b"])
    gla_o, gla_s = _gla(pr["gla"], pr["small"], lw["gla_w_lr2"], lw["gla_b_lr2"], states["gla"])
    ssd_y, ssd_s = _ssd(xbc, pr["small_cm"], lw["ssd_a_log"], lw["ssd_dt_bias"], states["ssd"])
    ml_h, ml_c, ml_nm = _mlstm(mqk, pr["ml_vo"], pr["small"], lw["ml_gate_bias"], states["ml_c"], states["ml_nm"])
    ug = _s5_rows_from_grid(pr["s5_chunks"]) if colmajor else _s5_rows_from_tokens(pr["s5"])
    y5, s5r, s5i = _s5(ug, lw["s5_mats"], states["s5_r"], states["s5_i"], bsz)
    y5 = _s5_rows_to_grid(y5, bsz) if colmajor else _s5_rows_to_tokens(y5, bsz)
    outs = dict(gla_o=gla_o, gla_proj=pr["gla"], ssd_y=ssd_y, ssd_xbc=xbc, ssd_z=pr["ssd_z"], ml_h=ml_h,
                ml_vo=pr["ml_vo"], s5_y=y5, s5_u=pr["s5"])
    finals = dict(gla=gla_s, ssd=ssd_s, ml_c=ml_c, ml_nm=ml_nm, s5_r=s5r, s5_i=s5i)
    return outs, finals


def _proj_weight(w):
    cols = [w[:, 0:1024], w[:, 1056:1568], w[:, 1568:1824], w[:, 1832:2344], w[:, 2344:2856], w[:, 2872:3128],
            w[:, 1024:1056], w[:, 1824:1832], w[:, 2856:2872], jnp.zeros((w.shape[0], SMALL_W - 56), w.dtype)]
    return jnp.concatenate(cols, axis=1).astype(BF16)


def kernel(x, c, ctx, c_ctx, w_mod, b_mod, g_norm, ffn_w_in, ffn_w_out, w_in, w_out, gla_w_lr2, gla_b_lr2,
           gla_g_norm, ssd_conv_w, ssd_conv_b, ssd_a_log, ssd_dt_bias, ssd_d, ssd_g_norm, ml_conv_w, ml_conv_b,
           ml_gate_bias, ml_g_norm, s5_a_re, s5_a_im, s5_log_step, s5_b_re, s5_b_im, s5_c_re, s5_c_im, s5_d,
           s5_w_glu, s5_b_glu, g_final):
    bsz, length, d = x.shape
    depth = w_mod.shape[0]
    assert bsz + 1 <= MOD_ROWS and d == D_MODEL and length % 1024 == 0 and ctx.shape[1] % SCAN_TB == 0
    cc = jnp.concatenate([c, c_ctx[None, :], jnp.zeros((MOD_ROWS - bsz - 1, d), F32)], axis=0)
    mod = _modulation(cc, w_mod, b_mod)
    avg = jnp.kron(jnp.eye(N_HEADS, dtype=F32), jnp.full((HEAD_DIM, HEAD_DIM), 1.0 / HEAD_DIM, F32)).astype(BF16)
    row = lambda t: t.reshape(1, -1).astype(F32)
    zero_states = dict(
        gla=jnp.zeros((bsz, 2, HW, HW), F32), ssd=jnp.zeros((bsz, 2, HW, HW), F32),
        ml_c=jnp.zeros((bsz, 2, HW, HW), F32), ml_nm=jnp.zeros((bsz, 2, 8, HW), F32),
        s5_r=jnp.zeros((2, bsz, S5_GROUPS * S5_STATE), F32), s5_i=jnp.zeros((2, bsz, S5_GROUPS * S5_STATE), F32))
    for l in range(depth):
        last = l == depth - 1
        mod_x = mod[l, :bsz].reshape(bsz, N_MOD, 1, d)
        mod_c = mod[l, bsz:bsz + 1].reshape(1, N_MOD, 1, d)
        wi = [ffn_w_in[l, j].astype(BF16) for j in range(2)]
        wo = [ffn_w_out[l, j].astype(BF16) for j in range(2)]
        g = [row(g_norm[l, j]) for j in range(3)]
        lw = dict(
            w_in=_proj_weight(w_in[l]), gla_w_lr2=gla_w_lr2[l], gla_b_lr2=gla_b_lr2[l][:, None, :],
            ssd_conv_w=ssd_conv_w[l], ssd_conv_b=row(ssd_conv_b[l]), ssd_a_log=ssd_a_log[l].astype(F32),
            ssd_dt_bias=ssd_dt_bias[l], ml_conv_w=ml_conv_w[l], ml_conv_b=row(ml_conv_b[l]),
            ml_gate_bias=ml_gate_bias[l],
            s5_mats=_s5_prep(s5_a_re[l], s5_a_im[l], s5_log_step[l], s5_b_re[l], s5_b_im[l], s5_c_re[l], s5_c_im[l]))
        pw = dict(gla_g=row(gla_g_norm[l]), ssd_d=row(jnp.repeat(ssd_d[l], HEAD_DIM)), ssd_g=row(ssd_g_norm[l]),
                  ml_g=row(ml_g_norm[l]), s5_d=row(s5_d[l]), w_glu=s5_w_glu[l].astype(BF16), b_glu=row(s5_b_glu[l]),
                  avg=avg, w_out=w_out[l].astype(BF16))
        x = _ffn(x, mod_x, 0, g[0], wi[0], wo[0])
        ctx = _ffn(ctx, mod_c, 0, g[0], wi[0], wo[0])
        outs_c, finals = _token_mixing(ctx, mod_c, g[1], lw, zero_states, colmajor=False)
        outs_x, _ = _token_mixing(x, mod_x, g[1], lw, finals, colmajor=True)
        x = _post(x, mod_x, pw=pw, colmajor=True, **outs_x)
        x = _ffn(x, mod_x, 2, g[2], wi[1], wo[1], g_final=row(g_final) if last else None)
        if not last:
            ctx = _post(ctx, mod_c, pw=pw, colmajor=False, **outs_c)
            ctx = _ffn(ctx, mod_c, 2, g[2], wi[1], wo[1])
    return x
```

```python
import functools

import jax
import jax.numpy as jnp
from jax import lax
from jax.experimental import pallas as pl
from jax.experimental.pallas import tpu as pltpu

F32 = jnp.float32
BF16 = jnp.bfloat16
HI = lax.Precision.HIGHEST

D_MODEL = 1024
D_FF = 2816
GRID_W = 64
GROUP_W = 256
HEAD_DIM = 64
N_HEADS = 4
CHUNK = 64
EPS = 1e-6
N_MOD = 9
GLA_RANK = 16
GLA_NORMALISER = 16.0
S5_CH = 16
S5_GROUPS = 16
S5_STATE = 64
S5_T = 16
S5_PAIRS = S5_GROUPS // 2
SMALL_W = 128
PROJ_W = 3072 + SMALL_W
MOD_ROWS = 8
VMEM_LIMIT = 56 * 1024 * 1024


def _params(sem, vmem=VMEM_LIMIT):
    return pltpu.CompilerParams(dimension_semantics=sem, vmem_limit_bytes=vmem)


def _mm(a, b):
    return jnp.dot(a.astype(BF16), b.astype(BF16), preferred_element_type=F32)


def _mm_nt(a, b):
    return lax.dot_general(a.astype(BF16), b.astype(BF16), (((1,), (1,)), ((), ())), preferred_element_type=F32)


def _mm_tn(a, b):
    return lax.dot_general(a.astype(BF16), b.astype(BF16), (((0,), (0,)), ((), ())), preferred_element_type=F32)


def _mm_hi(a, b):
    return jnp.dot(a, b, precision=HI, preferred_element_type=F32)


def _mm_tn_hi(a, b):
    return lax.dot_general(a, b, (((0,), (0,)), ((), ())), precision=HI, preferred_element_type=F32)


def _rms(t):
    return t * lax.rsqrt(jnp.mean(t * t, axis=-1, keepdims=True) + EPS)


def _modnorm(xv, g, shift, scale):
    return (_rms(xv) * g) * (1.0 + scale) + shift


def _const(shape):
    n = len(shape)
    return pl.BlockSpec(shape, lambda *_: (0,) * n, pipeline_mode=pl.Buffered(1))


def _mod_kernel(c_ref, w_ref, b_ref, o_ref):
    o_ref[...] = _mm_hi(jax.nn.silu(c_ref[...]), w_ref[...]) + b_ref[...]


def _modulation(cc, w_mod, b_mod):
    depth, d, n = w_mod.shape
    tn = 1024
    return pl.pallas_call(
        _mod_kernel, grid=(depth, n // tn),
        in_specs=[pl.BlockSpec((MOD_ROWS, d), lambda l, j: (0, 0)),
                  pl.BlockSpec((None, d, tn), lambda l, j: (l, 0, j)),
                  pl.BlockSpec((None, 1, tn), lambda l, j: (l, 0, j))],
        out_specs=pl.BlockSpec((None, MOD_ROWS, tn), lambda l, j: (l, 0, j)),
        out_shape=jax.ShapeDtypeStruct((depth, MOD_ROWS, n), F32),
        compiler_params=_params(("arbitrary", "arbitrary")), name="modulation",
    )(cc, w_mod, b_mod.reshape(depth, 1, n))


def _mod_spec(k, per_batch):
    if per_batch:
        return pl.BlockSpec((None, None, 1, D_MODEL), lambda b, i: (b, k, 0, 0))
    return pl.BlockSpec((None, None, 1, D_MODEL), lambda b, i: (0, k, 0, 0))


FF_SPLIT = 2


def _ffn_body(xv, sh_ref, sc_ref, gt_ref, g_ref, wi_ref, wo_ref, gf_ref):
    h = _modnorm(xv, g_ref[...], sh_ref[...], sc_ref[...]).astype(BF16)
    piece = D_FF // FF_SPLIT
    y = None
    for s in range(FF_SPLIT):
        gate = jnp.dot(h, wi_ref[:, s * piece:(s + 1) * piece], preferred_element_type=F32)
        up = jnp.dot(h, wi_ref[:, D_FF + s * piece:D_FF + (s + 1) * piece], preferred_element_type=F32)
        a = (jax.nn.silu(gate) * up).astype(BF16)
        t = jnp.dot(a, wo_ref[s * piece:(s + 1) * piece, :], preferred_element_type=F32)
        y = t if y is None else y + t
    out = xv + (0.5 * gt_ref[...]) * y
    if gf_ref is not None:
        out = _rms(out) * gf_ref[...]
    return out


def _ffn_kernel(x_ref, sh_ref, sc_ref, gt_ref, g_ref, wi_ref, wo_ref, *rest, final):
    o_ref = rest[-1]
    o_ref[...] = _ffn_body(x_ref[...], sh_ref, sc_ref, gt_ref, g_ref, wi_ref, wo_ref, rest[0] if final else None)


def _ffn(x, modv, j, g, w_in, w_out, g_final=None):
    bsz, length, d = x.shape
    tm = min(512, length)
    per_batch = modv.shape[0] > 1
    final = g_final is not None
    ins = [x, modv, modv, modv, g, w_in, w_out]
    specs = [pl.BlockSpec((None, tm, d), lambda b, i: (b, i, 0)),
             _mod_spec(3 * j, per_batch), _mod_spec(3 * j + 1, per_batch), _mod_spec(3 * j + 2, per_batch),
             _const((1, d)), _const((d, 2 * D_FF)), _const((D_FF, d))]
    if final:
        ins.append(g_final)
        specs.append(_const((1, d)))
    return pl.pallas_call(
        functools.partial(_ffn_kernel, final=final), grid=(bsz, length // tm),
        in_specs=specs, out_specs=pl.BlockSpec((None, tm, d), lambda b, i: (b, i, 0)),
        out_shape=jax.ShapeDtypeStruct(x.shape, F32),
        compiler_params=_params(("arbitrary", "arbitrary")), name="ffn",
    )(*ins)


_PROJ_SPLIT = (("gla", 0, 1024, False), ("ml_qk", 1792, 2304, False), ("ml_vo", 2304, 2816, False),
               ("small", 3072, 3200, False), ("ssd_xbc", 1024, 1536, True), ("ssd_z", 1536, 1792, True),
               ("s5", 2816, 3072, False), ("small_cm", 3072, 3200, True))
S5_HALF = S5_T // 2
S5_HALF_W = S5_PAIRS * S5_HALF * 2 * S5_CH


def _to_s5_chunks(ys):
    yt = ys.T
    pw = 2 * S5_CH
    outs = []
    for pp in range(S5_PAIRS):
        rows = yt[pp * pw:(pp + 1) * pw]
        z = jnp.concatenate([rows[:, s2 * 128:(s2 + 1) * 128] for s2 in range(S5_HALF // 2)], axis=0)
        zt = z.T
        outs.append(jnp.concatenate([zt[0:GRID_W], zt[GRID_W:2 * GRID_W]], axis=1))
    return jnp.concatenate(outs, axis=1)


def _from_s5_chunks(yc):
    pw = 2 * S5_CH
    cw = S5_HALF * pw
    rows = []
    for pp in range(S5_PAIRS):
        blk = yc[:, pp * cw:(pp + 1) * cw]
        z = jnp.concatenate([blk[:, 0:cw // 2], blk[:, cw // 2:cw]], axis=0).T
        rows.append(jnp.concatenate([z[s2 * pw:(s2 + 1) * pw] for s2 in range(S5_HALF // 2)], axis=1))
    return jnp.concatenate(rows, axis=0).T


def _inproj_kernel(x_ref, sh_ref, sc_ref, g_ref, w_ref, *outs, colmajor, rt):
    h = _modnorm(x_ref[...], g_ref[...], sh_ref[...], sc_ref[...]).astype(BF16)
    y = jnp.dot(h, w_ref[...], preferred_element_type=F32)
    for o_ref, (name, lo, hi, cm) in zip(outs, _PROJ_SPLIT):
        v = y[:, lo:hi]
        if cm and colmajor:
            v = jnp.swapaxes(v.reshape(rt, GRID_W, hi - lo), 0, 1)
        o_ref[...] = v
        if name == "s5" and colmajor:
            outs[-1][...] = _to_s5_chunks(v)


def _inproj(x, modv, g, w, colmajor):
    bsz, length, d = x.shape
    tm = min(512, length)
    rt = tm // GRID_W
    rows = length // GRID_W
    per_batch = modv.shape[0] > 1
    shapes, specs = [], []
    for _, lo, hi, cm in _PROJ_SPLIT:
        n = hi - lo
        if cm and colmajor:
            shapes.append(jax.ShapeDtypeStruct((bsz, GRID_W, rows, n), F32))
            specs.append(pl.BlockSpec((None, GRID_W, rt, n), lambda b, i: (b, 0, i, 0)))
        else:
            shapes.append(jax.ShapeDtypeStruct((bsz, length, n), F32))
            specs.append(pl.BlockSpec((None, tm, n), lambda b, i: (b, i, 0)))
    if colmajor:
        assert rt == S5_HALF and rows % S5_T == 0
        shapes.append(jax.ShapeDtypeStruct((bsz, rows // S5_T, GRID_W, 2 * S5_HALF_W), F32))
        specs.append(pl.BlockSpec((None, None, GRID_W, S5_HALF_W), lambda b, i: (b, i // 2, 0, i % 2)))
    outs = pl.pallas_call(
        functools.partial(_inproj_kernel, colmajor=colmajor, rt=rt), grid=(bsz, length // tm),
        in_specs=[pl.BlockSpec((None, tm, d), lambda b, i: (b, i, 0)),
                  _mod_spec(3, per_batch), _mod_spec(4, per_batch), _const((1, d)), _const((d, PROJ_W))],
        out_specs=specs, out_shape=shapes,
        compiler_params=_params(("arbitrary", "arbitrary")), name="inproj",
    )(x, modv, modv, g, w)
    res = {}
    for o, (name, lo, hi, cm) in zip(outs, _PROJ_SPLIT):
        res[name] = o.reshape(bsz, length, hi - lo)
    if colmajor:
        res["s5_chunks"] = outs[-1]
    return res


def _conv_kernel(x_ref, p_ref, n_ref, w_ref, b_ref, o_ref, *, tb):
    i = pl.program_id(1)
    nb = pl.num_programs(1)
    x = x_ref[...]
    rid = lax.broadcasted_iota(jnp.int32, x.shape, 0)
    prev = jnp.where(i > 0, p_ref[7:8, :], 0.0)
    nxt = jnp.where(i < nb - 1, n_ref[0:1, :], 0.0)
    xm = jnp.where(rid == 0, prev, pltpu.roll(x, 1, 0))
    xp = jnp.where(rid == tb - 1, nxt, pltpu.roll(x, tb - 1, 0))
    y = w_ref[0:1, :] * xm + w_ref[1:2, :] * x + w_ref[2:3, :] * xp + b_ref[...]
    o_ref[...] = jax.nn.silu(y)


def _conv_silu(x, w, b):
    bsz, length, ch = x.shape
    tb = min(1024, length)
    t8 = tb // 8
    last8 = length // 8 - 1
    return pl.pallas_call(
        functools.partial(_conv_kernel, tb=tb), grid=(bsz, length // tb),
        in_specs=[pl.BlockSpec((None, tb, ch), lambda bb, i: (bb, i, 0)),
                  pl.BlockSpec((None, 8, ch), lambda bb, i: (bb, jnp.maximum(i * t8 - 1, 0), 0)),
                  pl.BlockSpec((None, 8, ch), lambda bb, i: (bb, jnp.minimum((i + 1) * t8, last8), 0)),
                  _const((3, ch)), _const((1, ch))],
        out_specs=pl.BlockSpec((None, tb, ch), lambda bb, i: (bb, i, 0)),
        out_shape=jax.ShapeDtypeStruct(x.shape, F32),
        compiler_params=_params(("arbitrary", "arbitrary")), name="conv_silu",
    )(x, x, x, w, b)


HW = N_HEADS * HEAD_DIM
CPB = 4
SCAN_TB = CPB * CHUNK


def _scan_block(d, i, nb):
    return jnp.where(d == 0, i, nb - 1 - i)


def _chunk_rows(d, cc):
    c = jnp.where(d == 0, cc, CPB - 1 - cc)
    return pl.ds(pl.multiple_of(c * CHUNK, CHUNK), CHUNK)


def _scan_consts(d):
    t = lax.broadcasted_iota(jnp.int32, (CHUNK, HW), 0)
    lane = lax.broadcasted_iota(jnp.int32, (CHUNK, HW), 1)
    s = jnp.bitwise_and(lane, HEAD_DIM - 1)
    vis = jnp.where(d == 0, t - s, s - t)
    r64 = lax.broadcasted_iota(jnp.int32, (CHUNK, CHUNK), 0)
    c64 = lax.broadcasted_iota(jnp.int32, (CHUNK, CHUNK), 1)
    rb = lax.broadcasted_iota(jnp.int32, (HW, HW), 0) // HEAD_DIM
    cb = lax.broadcasted_iota(jnp.int32, (HW, HW), 1) // HEAD_DIM
    return dict(
        mask4=vis >= 0,
        maskt4=(vis <= 0).astype(F32),
        eye4=(vis == 0).astype(F32),
        cumsum=(jnp.where(d == 0, r64 - c64, c64 - r64) >= 0).astype(BF16),
        bd=(rb == cb).astype(BF16),
        head=lane // HEAD_DIM)


def _split(x, terms):
    parts = []
    for _ in range(terms - 1):
        parts.append(x.astype(BF16))
        x = x - parts[-1].astype(F32)
    return parts + [x.astype(BF16)]


def _mm_r01(x, w01, terms=2):
    return sum(jnp.dot(p, w01, preferred_element_type=F32) for p in _split(x, terms))


def _mm_l01(w01, x, terms=3):
    return sum(jnp.dot(w01, p, preferred_element_type=F32) for p in _split(x, terms))


def _bd(x, bd01):
    return jnp.concatenate([x.astype(BF16)] * N_HEADS, axis=0) * bd01


def _scan_specs(nb, bsz):
    def blk(width, colblock):
        return pl.BlockSpec((bsz, SCAN_TB, width), lambda d, i: (0, _scan_block(d, i, nb), colblock))
    out = pl.BlockSpec((None, bsz, SCAN_TB, GROUP_W), lambda d, i: (d, 0, _scan_block(d, i, nb), 0))
    state = pl.BlockSpec((bsz, None, HW, HW), lambda d, i: (0, d, 0, 0))
    return blk, out, state


def _dir_spec(shape):
    n = len(shape)
    return pl.BlockSpec((None,) + shape, lambda d, i: (d,) + (0,) * n)


def _gla_kernel(q_ref, k_ref, v_ref, sm_ref, w_ref, b_ref, s0_ref, o_ref, sf_ref, s_ref, *, bsz):
    d = pl.program_id(0)
    i = pl.program_id(1)

    @pl.when(i == 0)
    def _():
        s_ref[...] = s0_ref[...]

    cs = _scan_consts(d)
    bs = range(bsz)
    for cc in range(CPB):
        rows = _chunk_rows(d, cc)
        sm = sm_ref[:, rows, :].reshape(bsz * CHUNK, SMALL_W)
        lr = jnp.where(d == 0, sm[:, 0:GLA_RANK], sm[:, GLA_RANK:2 * GLA_RANK])
        g = jax.nn.log_sigmoid(_mm(lr, w_ref[...]) + b_ref[...]) * (1.0 / GLA_NORMALISER)
        g = [g[b * CHUNK:(b + 1) * CHUNK] for b in bs]
        bc = _mm_l01(cs["cumsum"], jnp.concatenate(g, axis=1))
        bc = [bc[:, b * HW:(b + 1) * HW] for b in bs]
        bt = [jnp.sum(g[b], axis=0, keepdims=True) for b in bs]
        k = [k_ref[b, rows, :] for b in bs]
        v = [v_ref[b, rows, :].astype(BF16) for b in bs]
        qt = [(q_ref[b, rows, :] * (HEAD_DIM ** -0.5) * jnp.exp(bc[b])).astype(BF16) for b in bs]
        st = [s_ref[b] for b in bs]
        o_st = [_mm_nt(qt[b], st[b].astype(BF16) * cs["bd"]) for b in bs]
        kbd = [_bd(k[b] * jnp.exp(-bc[b]), cs["bd"]) for b in bs]
        att = [jnp.where(cs["mask4"], _mm_nt(qt[b], kbd[b]), 0.0) for b in bs]
        ks = [(k[b] * jnp.exp(bt[b] - bc[b])).astype(BF16) for b in bs]
        upd = [_mm_tn(v[b], ks[b]) for b in bs]
        vbd = [_bd(v[b], cs["bd"]) for b in bs]
        for b in bs:
            o_ref[b, rows, :] = _mm(att[b], vbd[b]) + o_st[b]
        for b in bs:
            s_ref[b] = st[b] * jnp.exp(bt[b]) + upd[b]

    @pl.when(i == pl.num_programs(1) - 1)
    def _():
        sf_ref[...] = s_ref[...]


def _gla(proj, small, w_lr2, b_lr2, s0):
    bsz, length, _ = proj.shape
    nb = length // SCAN_TB
    blk, out, state = _scan_specs(nb, bsz)
    return pl.pallas_call(
        functools.partial(_gla_kernel, bsz=bsz), grid=(2, nb),
        in_specs=[blk(GROUP_W, 0), blk(GROUP_W, 1), blk(GROUP_W, 2), blk(SMALL_W, 0),
                  _dir_spec((GLA_RANK, GROUP_W)), _dir_spec((1, GROUP_W)), state],
        out_specs=[out, state],
        out_shape=[jax.ShapeDtypeStruct((2, bsz, length, GROUP_W), F32), jax.ShapeDtypeStruct(s0.shape, F32)],
        scratch_shapes=[pltpu.VMEM((bsz, HW, HW), F32)],
        compiler_params=_params(("arbitrary",) * 2), name="gla",
    )(proj, proj, proj, small, w_lr2, b_lr2, s0)


def _ssd_kernel(x_ref, b_ref, c_ref, sm_ref, al_ref, db_ref, s0_ref, o_ref, sf_ref, s_ref, *, bsz):
    d = pl.program_id(0)
    i = pl.program_id(1)

    @pl.when(i == 0)
    def _():
        s_ref[...] = s0_ref[...]

    cs = _scan_consts(d)
    r = lax.broadcasted_iota(jnp.int32, (SMALL_W, HW), 0)
    c = lax.broadcasted_iota(jnp.int32, (SMALL_W, HW), 1)
    e_dt = (r == 32 + N_HEADS * d + c // HEAD_DIM).astype(BF16)
    e_grp = ((r // HEAD_DIM == c // (2 * HEAD_DIM)) & (r % HEAD_DIM == c % HEAD_DIM)).astype(BF16)
    rg = lax.broadcasted_iota(jnp.int32, (HW, 2 * HEAD_DIM), 0) // (2 * HEAD_DIM)
    cg = lax.broadcasted_iota(jnp.int32, (HW, 2 * HEAD_DIM), 1) // HEAD_DIM
    grp = (rg == cg).astype(BF16)
    nega = -jnp.exp(al_ref[...])
    bs = range(bsz)
    for cc in range(CPB):
        rows = _chunk_rows(d, cc)
        sm = sm_ref[:, rows, :].reshape(bsz * CHUNK, SMALL_W)
        dt = jax.nn.softplus(_mm_r01(sm, e_dt) + db_ref[...])
        a = dt * nega
        dt = [dt[b * CHUNK:(b + 1) * CHUNK] for b in bs]
        a = [a[b * CHUNK:(b + 1) * CHUNK] for b in bs]
        cum = _mm_l01(cs["cumsum"], jnp.concatenate(a, axis=1))
        cum = [cum[:, b * HW:(b + 1) * HW] for b in bs]
        bm = b_ref[:, rows, :].reshape(bsz * CHUNK, 2 * HEAD_DIM).astype(BF16)
        cm = c_ref[:, rows, :].reshape(bsz * CHUNK, 2 * HEAD_DIM).astype(BF16)
        cmx = _mm(cm, e_grp)
        bmx = _mm(bm, e_grp)
        bm = [bm[b * CHUNK:(b + 1) * CHUNK] for b in bs]
        cm = [cm[b * CHUNK:(b + 1) * CHUNK] for b in bs]
        st = [s_ref[b] for b in bs]
        y_st = [_mm(cmx[b * CHUNK:(b + 1) * CHUNK], st[b].astype(BF16) * cs["bd"]) for b in bs]
        cb4 = [_mm_nt(cm[b], jnp.concatenate([bm[b]] * N_HEADS, axis=0) * grp) for b in bs]
        tot = [jnp.sum(a[b], axis=0, keepdims=True) for b in bs]
        cumr = [jnp.sum(a[b] * cs["maskt4"], axis=0, keepdims=True) for b in bs]
        xdt = [(x_ref[b, rows, :] * dt[b]).astype(BF16) for b in bs]
        bw = [(bmx[b * CHUNK:(b + 1) * CHUNK] * jnp.exp(tot[b] - cum[b])).astype(BF16) for b in bs]
        upd = [_mm_tn(bw[b], xdt[b]) for b in bs]
        xbd = [_bd(xdt[b], cs["bd"]) for b in bs]
        sc = [cb4[b] * jnp.exp(jnp.where(cs["mask4"], cum[b] - cumr[b], -jnp.inf)) for b in bs]
        for b in bs:
            o_ref[b, rows, :] = _mm(sc[b], xbd[b]) + jnp.exp(cum[b]) * y_st[b]
        for b in bs:
            s_ref[b] = st[b] * jnp.exp(tot[b]) + upd[b]

    @pl.when(i == pl.num_programs(1) - 1)
    def _():
        sf_ref[...] = s_ref[...]


def _per_head_lanes(t):
    return jnp.repeat(t.astype(F32), HEAD_DIM, axis=-1)[:, None, :]


def _ssd(xbc, small, a_log, dt_bias, s0):
    bsz, length, _ = xbc.shape
    nb = length // SCAN_TB
    blk, out, state = _scan_specs(nb, bsz)
    return pl.pallas_call(
        functools.partial(_ssd_kernel, bsz=bsz), grid=(2, nb),
        in_specs=[blk(GROUP_W, 0), blk(128, 2), blk(128, 3), blk(SMALL_W, 0),
                  _dir_spec((1, HW)), _dir_spec((1, HW)), state],
        out_specs=[out, state],
        out_shape=[jax.ShapeDtypeStruct((2, bsz, length, GROUP_W), F32), jax.ShapeDtypeStruct(s0.shape, F32)],
        scratch_shapes=[pltpu.VMEM((bsz, HW, HW), F32)],
        compiler_params=_params(("arbitrary",) * 2), name="ssd",
    )(xbc, xbc, xbc, small, _per_head_lanes(a_log), _per_head_lanes(dt_bias), s0)


def _mlstm_kernel(q_ref, k_ref, v_ref, sm_ref, gi_ref, gf_ref, c0_ref, nm0_ref, o_ref, cf_ref, nmf_ref,
                  c_ref, nm_ref, *, bsz):
    d = pl.program_id(0)
    i = pl.program_id(1)

    @pl.when(i == 0)
    def _():
        c_ref[...] = c0_ref[...]
        nm_ref[...] = nm0_ref[...]

    cs = _scan_consts(d)
    r = lax.broadcasted_iota(jnp.int32, (SMALL_W, 2 * HW), 0)
    c = lax.broadcasted_iota(jnp.int32, (SMALL_W, 2 * HW), 1)
    gate = 40 + 2 * N_HEADS * d + N_HEADS * (c // HW) + (c % HW) // HEAD_DIM
    e_if = (r == gate).astype(BF16)
    rid = lax.broadcasted_iota(jnp.int32, (CHUNK, 1), 0)
    last = rid == jnp.where(d == 0, CHUNK - 1, 0)
    bs = range(bsz)

    def per_b(t):
        return [t[b * CHUNK:(b + 1) * CHUNK] for b in bs]

    for cc in range(CPB):
        rows = _chunk_rows(d, cc)
        sm = sm_ref[:, rows, :].reshape(bsz * CHUNK, SMALL_W)
        gates = _mm_r01(sm, e_if)
        gi = per_b(gates[:, :HW] + gi_ref[...])
        lf = per_b(jax.nn.log_sigmoid(gates[:, HW:] + gf_ref[...]))
        fc = _mm_l01(cs["cumsum"], jnp.concatenate(lf, axis=1))
        fc = [fc[:, b * HW:(b + 1) * HW] for b in bs]
        q = [q_ref[b, rows, :] for b in bs]
        qb = [q[b].astype(BF16) for b in bs]
        k = [k_ref[b, rows, :] * (HEAD_DIM ** -0.5) for b in bs]
        v = [v_ref[b, rows, :].astype(BF16) for b in bs]
        cst = [c_ref[b] for b in bs]
        ns = [nm_ref[b, 0:1, :] for b in bs]
        ms = [nm_ref[b, 1:2, :] for b in bs]
        n_st = [_mm(qb[b], cst[b].astype(BF16) * cs["bd"]) for b in bs]
        d_st = [_mm_r01(q[b] * ns[b], cs["bd"]) for b in bs]
        qk = [_mm_nt(qb[b], _bd(k[b], cs["bd"])) for b in bs]
        ftot = [jnp.sum(lf[b], axis=0, keepdims=True) for b in bs]
        fr = [jnp.sum(lf[b] * cs["maskt4"], axis=0, keepdims=True) for b in bs]
        ir = [jnp.sum(gi[b] * cs["eye4"], axis=0, keepdims=True) for b in bs]
        li = [jnp.where(cs["mask4"], fc[b] - fr[b] + ir[b], -jnp.inf) for b in bs]
        m = []
        for b in bs:
            rmax = jnp.zeros_like(li[b])
            for h in range(N_HEADS):
                mh = jnp.max(li[b][:, h * HEAD_DIM:(h + 1) * HEAD_DIM], axis=-1, keepdims=True)
                rmax = jnp.where(cs["head"] == h, mh, rmax)
            m.append(jnp.maximum(fc[b] + ms[b], rmax))
        m_new = [jnp.sum(jnp.where(last, m[b], 0.0), axis=0, keepdims=True) for b in bs]
        kw = [(k[b] * jnp.exp(ftot[b] - fc[b] + gi[b] - m_new[b])) for b in bs]
        upd = [_mm_tn(kw[b], v[b]) for b in bs]
        wf = [qk[b] * jnp.exp(li[b] - m[b]) for b in bs]
        w = [wf[b].astype(BF16) for b in bs]
        w_lo = [(wf[b] - w[b].astype(F32)).astype(BF16) for b in bs]
        vbd = [_bd(v[b], cs["bd"]) for b in bs]
        n_in = [_mm(w[b], vbd[b]) for b in bs]
        d_in = [_mm(w[b], cs["bd"]) + _mm(w_lo[b], cs["bd"]) for b in bs]
        for b in bs:
            winter = jnp.exp(fc[b] + ms[b] - m[b])
            den = d_in[b] + winter * d_st[b]
            o_ref[b, rows, :] = (n_in[b] + winter * n_st[b]) / jnp.maximum(jnp.abs(den), jnp.exp(-m[b]))
        for b in bs:
            decay = jnp.exp(ftot[b] + ms[b] - m_new[b])
            c_ref[b] = cst[b] * decay + upd[b]
            nm_ref[b, 0:1, :] = decay * ns[b] + jnp.sum(kw[b], axis=0, keepdims=True)
            nm_ref[b, 1:2, :] = m_new[b]

    @pl.when(i == pl.num_programs(1) - 1)
    def _():
        cf_ref[...] = c_ref[...]
        nmf_ref[...] = nm_ref[...]


def _mlstm(qk, vo, small, gate_bias, c0, nm0):
    bsz, length, _ = qk.shape
    nb = length // SCAN_TB
    blk, out, state = _scan_specs(nb, bsz)
    nm_spec = pl.BlockSpec((bsz, None, 8, HW), lambda d, i: (0, d, 0, 0))
    return pl.pallas_call(
        functools.partial(_mlstm_kernel, bsz=bsz), grid=(2, nb),
        in_specs=[blk(GROUP_W, 0), blk(GROUP_W, 1), blk(GROUP_W, 0), blk(SMALL_W, 0),
                  _dir_spec((1, HW)), _dir_spec((1, HW)), state, nm_spec],
        out_specs=[out, state, nm_spec],
        out_shape=[jax.ShapeDtypeStruct((2, bsz, length, GROUP_W), F32), jax.ShapeDtypeStruct(c0.shape, F32),
                   jax.ShapeDtypeStruct(nm0.shape, F32)],
        scratch_shapes=[pltpu.VMEM((bsz, HW, HW), F32), pltpu.VMEM((bsz, 8, HW), F32)],
        compiler_params=_params(("arbitrary",) * 2), name="mlstm",
    )(qk, qk, vo, small, _per_head_lanes(gate_bias[:, 0]), _per_head_lanes(gate_bias[:, 1]), c0, nm0)


def _s5_slot(s, gl):
    return (s // S5_HALF) * (S5_HALF * 2 * S5_CH) + (s % 2) * 128 + ((s % S5_HALF) // 2) * 2 * S5_CH + gl * S5_CH


def _s5_prep_kernel(arc, aic, arr, air, lst, brn, bin_, bri, bii, ctr, cti, m_ref, q_ref, p_ref, a16r_ref, a16i_ref):
    n, t16 = S5_STATE, S5_T
    gw = S5_T * S5_CH
    lane_t = lax.broadcasted_iota(jnp.int32, (n, gw), 1) // S5_CH
    sel = (lax.broadcasted_iota(jnp.int32, (S5_CH, gw), 1) % S5_CH
           == lax.broadcasted_iota(jnp.int32, (S5_CH, gw), 0)).astype(F32)
    lane_blk = lax.broadcasted_iota(jnp.int32, (S5_CH, gw), 1) // S5_CH
    lane128 = lax.broadcasted_iota(jnp.int32, (S5_CH, 2 * n), 1)
    m_ref[...] = jnp.zeros_like(m_ref)
    q_ref[...] = jnp.zeros_like(q_ref)
    p_ref[...] = jnp.zeros_like(p_ref)

    def discretise(a_re, a_im, step):
        a_re = jnp.minimum(a_re, -1e-4)
        mag = jnp.exp(a_re * step)
        ab_re, ab_im = mag * jnp.cos(a_im * step), mag * jnp.sin(a_im * step)
        den = a_re * a_re + a_im * a_im
        nr, ni = ab_re - 1.0, ab_im
        return ab_re, ab_im, (nr * a_re + ni * a_im) / den, (ni * a_re - nr * a_im) / den

    def powers(ab_re, ab_im):
        pr, pi = [jnp.ones_like(ab_re)], [jnp.zeros_like(ab_re)]
        for _ in range(t16):
            pr.append(pr[-1] * ab_re - pi[-1] * ab_im)
            pi.append(pr[-2] * ab_im + pi[-1] * ab_re)
        return pr, pi

    for d in range(2):
        step_r = jnp.exp(jnp.concatenate([jnp.broadcast_to(lst[d, gl], (1, n)) for gl in range(2)], axis=1))
        ab_re, ab_im, cf_re, cf_im = discretise(arr[d], air[d], step_r)
        prr, pir = powers(ab_re, ab_im)
        a16r_ref[d] = prr[t16]
        a16i_ref[d] = pir[t16]
        bt_re = cf_re * bri[...] - cf_im * bii[...]
        bt_im = cf_re * bii[...] + cf_im * bri[...]
        for s in range(t16):
            e = t16 - 1 - s if d == 0 else s
            qre = bt_re * prr[e] - bt_im * pir[e]
            qim = bt_re * pir[e] + bt_im * prr[e]
            for gl in range(2):
                keep = (lane128 // n) == gl
                r0 = _s5_slot(s, gl)
                q_ref[d, r0:r0 + S5_CH, 0:2 * n] = jnp.where(keep, qre, 0.0).astype(q_ref.dtype)
                q_ref[d, r0:r0 + S5_CH, 2 * n:4 * n] = jnp.where(keep, qim, 0.0).astype(q_ref.dtype)
        for gl in range(2):
            ab_re, ab_im, cf_re, cf_im = discretise(arc[d, gl], aic[d, gl], jnp.exp(lst[d, gl]))
            prc, pic = powers(ab_re, ab_im)
            bb_re = cf_re * brn[gl] - cf_im * bin_[gl]
            bb_im = cf_re * bin_[gl] + cf_im * brn[gl]
            ct_re = _mm_hi(ctr[gl], sel)
            ct_im = _mm_hi(cti[gl], sel)

            def response(exps):
                p_re = jnp.zeros((n, gw), F32)
                p_im = jnp.zeros((n, gw), F32)
                for t in range(t16):
                    p_re = jnp.where(lane_t == t, prc[exps[t]], p_re)
                    p_im = jnp.where(lane_t == t, pic[exps[t]], p_im)
                return ct_re * p_re - ct_im * p_im, ct_re * p_im + ct_im * p_re

            e_re, e_im = response([t if d == 0 else t16 - 1 - t for t in range(t16)])
            r0v = _mm_tn_hi(bb_re, e_re) - _mm_tn_hi(bb_im, e_im)
            for s in range(t16):
                if d == 0:
                    blk = jnp.where(lane_blk >= s, pltpu.roll(r0v, (S5_CH * s) % gw, 1), 0.0)
                else:
                    blk = jnp.where(lane_blk <= s, pltpu.roll(r0v, (gw - S5_CH * (t16 - 1 - s)) % gw, 1), 0.0)
                r0 = _s5_slot(s, gl)
                m_ref[d, r0:r0 + S5_CH, gl * gw:(gl + 1) * gw] = blk.astype(m_ref.dtype)
            c_re, c_im = response([t + 1 if d == 0 else t16 - t for t in range(t16)])
            p_ref[d, gl * n:(gl + 1) * n, gl * gw:(gl + 1) * gw] = c_re.astype(p_ref.dtype)
            p_ref[d, 2 * n + gl * n:2 * n + (gl + 1) * n, gl * gw:(gl + 1) * gw] = (-c_im).astype(p_ref.dtype)
    c = lax.broadcasted_iota(jnp.int32, (2 * gw, 2 * gw), 1)
    half = S5_HALF * 2 * S5_CH
    t_of_c = S5_HALF * (c // half) + 2 * ((c % 128) // (2 * S5_CH)) + (c % half) // 128
    src = ((c % (2 * S5_CH)) // S5_CH) * gw + t_of_c * S5_CH + c % S5_CH
    perm = (lax.broadcasted_iota(jnp.int32, (2 * gw, 2 * gw), 0) == src).astype(BF16)
    for d in range(2):
        m_ref[d] = jnp.dot(m_ref[d], perm, preferred_element_type=F32).astype(m_ref.dtype)
        p_ref[d] = jnp.dot(p_ref[d], perm, preferred_element_type=F32).astype(p_ref.dtype)


def _s5_prep(a_re, a_im, log_step, b_re, b_im, c_re, c_im):
    n, g, ch, pr = S5_STATE, S5_GROUPS, S5_CH, S5_PAIRS
    f = lambda t: t.astype(F32)
    arc, aic = f(a_re)[..., None], f(a_im)[..., None]
    arr, air = f(a_re).reshape(2, pr, 1, 2 * n), f(a_im).reshape(2, pr, 1, 2 * n)
    lst = f(log_step).reshape(2, g, 1, 1)
    pair_rows = lambda t: f(t).reshape(pr, 2, n, ch).transpose(0, 3, 1, 2).reshape(pr, ch, 2 * n)
    ctr, cti = f(c_re).transpose(0, 2, 1), f(c_im).transpose(0, 2, 1)
    gw2 = 2 * S5_T * S5_CH
    spec_c = pl.BlockSpec((2, 2, n, 1), lambda p: (0, p, 0, 0))
    spec_r = pl.BlockSpec((2, None, 1, 2 * n), lambda p: (0, p, 0, 0))
    spec_g = pl.BlockSpec((2, n, ch), lambda p: (p, 0, 0))
    spec_p = pl.BlockSpec((None, ch, 2 * n), lambda p: (p, 0, 0))
    outs = pl.pallas_call(
        _s5_prep_kernel, grid=(pr,),
        in_specs=[spec_c, spec_c, spec_r, spec_r, pl.BlockSpec((2, 2, 1, 1), lambda p: (0, p, 0, 0)),
                  spec_g, spec_g, spec_p, spec_p, spec_g, spec_g],
        out_specs=[pl.BlockSpec((2, None, gw2, gw2), lambda p: (0, p, 0, 0)),
                   pl.BlockSpec((2, None, gw2, 4 * n), lambda p: (0, p, 0, 0)),
                   pl.BlockSpec((2, None, 4 * n, gw2), lambda p: (0, p, 0, 0)),
                   pl.BlockSpec((2, None, 1, 2 * n), lambda p: (0, p, 0, 0)),
                   pl.BlockSpec((2, None, 1, 2 * n), lambda p: (0, p, 0, 0))],
        out_shape=[jax.ShapeDtypeStruct((2, pr, gw2, gw2), BF16), jax.ShapeDtypeStruct((2, pr, gw2, 4 * n), BF16),
                   jax.ShapeDtypeStruct((2, pr, 4 * n, gw2), BF16),
                   jax.ShapeDtypeStruct((2, pr, 1, 2 * n), F32), jax.ShapeDtypeStruct((2, pr, 1, 2 * n), F32)],
        compiler_params=_params(("arbitrary",)), name="s5_prep",
    )(arc, aic, arr, air, lst, f(b_re), f(b_im), pair_rows(b_re), pair_rows(b_im), ctr, cti)
    m, q, p, a16r, a16i = outs
    return m, q, p, a16r.reshape(2, 1, g * n), a16i.reshape(2, 1, g * n)


def _s5_chunk_rows(u0_ref, u1_ref):
    u = jnp.concatenate([u0_ref[...], u1_ref[...]], axis=-1)
    r, w, lanes = u.shape
    return jnp.swapaxes(u, 0, 1).reshape(w * r, lanes).astype(BF16)


def _s5_state_kernel(u0_ref, u1_ref, q_ref, vr_ref, vi_ref):
    v = jnp.dot(_s5_chunk_rows(u0_ref, u1_ref), q_ref[...], preferred_element_type=F32)
    vr_ref[...] = v[:, 0:128]
    vi_ref[...] = v[:, 128:256]


def _s5_scan_kernel(vr_ref, vi_ref, ar_ref, ai_ref, h0r_ref, h0i_ref, hr_ref, hi_ref, fr_ref, fi_ref, *, nch, bsz):
    d = pl.program_id(0)
    ar = jnp.broadcast_to(ar_ref[...], h0r_ref.shape)
    ai = jnp.broadcast_to(ai_ref[...], h0r_ref.shape)
    cpt = 8 // bsz
    ntile = nch // cpt

    def run(reverse):
        def body(kk, carry):
            hr, hi = carry
            t = ntile - 1 - kk if reverse else kk
            rows = pl.ds(pl.multiple_of(t * 8, 8), 8)
            vr8, vi8 = vr_ref[rows, :], vi_ref[rows, :]
            in_r, in_i = [None] * cpt, [None] * cpt
            for j in (range(cpt - 1, -1, -1) if reverse else range(cpt)):
                in_r[j], in_i[j] = hr, hi
                sl = slice(j * bsz, (j + 1) * bsz)
                hr, hi = ar * hr - ai * hi + vr8[sl], ar * hi + ai * hr + vi8[sl]
            hr_ref[rows, :] = jnp.concatenate(in_r, axis=0)
            hi_ref[rows, :] = jnp.concatenate(in_i, axis=0)
            return hr, hi

        hr, hi = lax.fori_loop(0, ntile, body, (h0r_ref[...], h0i_ref[...]))
        fr_ref[...] = hr
        fi_ref[...] = hi

    @pl.when(d == 0)
    def _():
        run(False)

    @pl.when(d == 1)
    def _():
        run(True)


def _s5_out_kernel(u0_ref, u1_ref, m_ref, p_ref, hr_ref, hi_ref, y_ref):
    u = _s5_chunk_rows(u0_ref, u1_ref)
    y = None
    for d in range(2):
        hcat = jnp.concatenate([hr_ref[d], hi_ref[d]], axis=-1).astype(BF16)
        t = jnp.dot(u, m_ref[d], preferred_element_type=F32) + jnp.dot(hcat, p_ref[d], preferred_element_type=F32)
        y = t if y is None else y + t
    r, w, lanes = y_ref.shape
    y_ref[...] = jnp.swapaxes(y.reshape(w, r, lanes), 0, 1)


def _s5(uc, mats, h0r, h0i):
    m, q, p, a16r, a16i = mats
    bsz, r, w, _ = uc.shape
    nch = r * w
    gn = S5_GROUPS * S5_STATE
    gw2 = 2 * S5_T * S5_CH
    hw = gw2 // 2

    def u_spec(half, order):
        def imap(*idx):
            b, pp = order(*idx)
            return (b, 0, 0, half * S5_PAIRS + pp)
        return pl.BlockSpec((None, r, w, hw), imap)

    sb = lambda d, pp, b: (b, pp)
    vr, vi = pl.pallas_call(
        _s5_state_kernel, grid=(2, S5_PAIRS, bsz),
        in_specs=[u_spec(0, sb), u_spec(1, sb),
                  pl.BlockSpec((None, None, gw2, 256), lambda d, pp, b: (d, pp, 0, 0))],
        out_specs=[pl.BlockSpec((None, nch, 128), lambda d, pp, b: (d, 0, b * S5_PAIRS + pp))] * 2,
        out_shape=[jax.ShapeDtypeStruct((2, nch, bsz * gn), F32)] * 2,
        compiler_params=_params(("arbitrary",) * 3), name="s5_state",
    )(uc, uc, q)
    lb = 256
    blk = pl.BlockSpec((None, nch, lb), lambda d, j: (d, 0, j))
    vec = pl.BlockSpec((None, 1, lb), lambda d, j: (d, 0, j))
    hr, hi, fr, fi = pl.pallas_call(
        functools.partial(_s5_scan_kernel, nch=nch, bsz=1), grid=(2, bsz * gn // lb),
        in_specs=[blk, blk, vec, vec, vec, vec], out_specs=[blk, blk, vec, vec],
        out_shape=[jax.ShapeDtypeStruct((2, nch, bsz * gn), F32)] * 2 + [jax.ShapeDtypeStruct((2, 1, bsz * gn), F32)] * 2,
        compiler_params=_params(("arbitrary", "arbitrary")), name="s5_scan",
    )(vr, vi, jnp.tile(a16r, (1, 1, bsz)), jnp.tile(a16i, (1, 1, bsz)), h0r, h0i)
    ob = lambda pp, hf, b: (b, pp)
    y = pl.pallas_call(
        _s5_out_kernel, grid=(S5_PAIRS, 2, bsz),
        in_specs=[u_spec(0, ob), u_spec(1, ob),
                  pl.BlockSpec((2, None, gw2, hw), lambda pp, hf, b: (0, pp, 0, hf)),
                  pl.BlockSpec((2, None, 256, hw), lambda pp, hf, b: (0, pp, 0, hf)),
                  pl.BlockSpec((2, nch, 128), lambda pp, hf, b: (0, 0, b * S5_PAIRS + pp)),
                  pl.BlockSpec((2, nch, 128), lambda pp, hf, b: (0, 0, b * S5_PAIRS + pp))],
        out_specs=pl.BlockSpec((None, r, w, hw), lambda pp, hf, b: (b, 0, 0, hf * S5_PAIRS + pp)),
        out_shape=jax.ShapeDtypeStruct(uc.shape, F32),
        compiler_params=_params(("arbitrary",) * 3), name="s5_out",
    )(uc, uc, m, p, hr, hi)
    return y, fr, fi


def _s5_rows_from_tokens(u):
    bsz, length, _ = u.shape
    nch = length // S5_T
    t = u.reshape(bsz, nch, 2, S5_HALF // 2, 2, S5_PAIRS, 2, S5_CH)
    t = t.transpose(0, 1, 2, 5, 4, 3, 6, 7)
    return t.reshape(bsz, 1, nch, 2 * S5_HALF_W)


def _s5_rows_to_tokens(y):
    bsz, _, nch, _ = y.shape
    t = y.reshape(bsz, nch, 2, S5_PAIRS, 2, S5_HALF // 2, 2, S5_CH)
    t = t.transpose(0, 1, 2, 5, 4, 3, 6, 7)
    return t.reshape(bsz, nch * S5_T, S5_GROUPS * S5_CH)


def _post_kernel(x_ref, gt_ref, go_ref, gr_ref, sy_ref, sx_ref, sz_ref, mh_ref, mo_ref, y5_ref, u5_ref,
                 gg_ref, sd_ref, sg_ref, mg_ref, d5_ref, wg_ref, bg_ref, avg_ref, wo_ref, *rest, colmajor, rt, final):
    o_ref = rest[-1]

    def rm(v):
        if not colmajor:
            return v
        return jnp.swapaxes(v, 0, 1).reshape(rt * GRID_W, v.shape[-1])

    def head_rms(t):
        return t * lax.rsqrt(_mm_r01(t * t, avg_ref[...], terms=3) + EPS)

    a = head_rms(go_ref[0] + go_ref[1]) * gg_ref[...] * jax.nn.silu(gr_ref[...])
    ys = rm(sy_ref[0] + sy_ref[1] + sd_ref[...] * sx_ref[...])
    b = _rms(ys * jax.nn.silu(rm(sz_ref[...]))) * sg_ref[...]
    m = head_rms(mh_ref[0] + mh_ref[1]) * mg_ref[...] * jax.nn.sigmoid(mo_ref[...])
    y5 = (_from_s5_chunks(y5_ref[...]) if colmajor else y5_ref[...]) + d5_ref[...] * u5_ref[...]
    gl = jax.nn.gelu(y5)
    dd = gl * jax.nn.sigmoid(_mm(gl, wg_ref[...]) + bg_ref[...])
    cat = jnp.concatenate([a, b, m, dd], axis=-1).astype(BF16)
    x1 = x_ref[...] + gt_ref[...] * jnp.dot(cat, wo_ref[...], preferred_element_type=F32)
    o_ref[...] = _ffn_body(x1, *rest[:6], rest[6] if final else None)


def _post(x, modv, gla_o, gla_proj, ssd_y, ssd_xbc, ssd_z, ml_h, ml_vo, s5_y, s5_u, pw, colmajor, ffn):
    bsz, length, d = x.shape
    tm = min(512, length)
    rt = tm // GRID_W
    rows = length // GRID_W
    per_batch = modv.shape[0] > 1
    w = GROUP_W

    def rmspec(colblock, dirs=False):
        if dirs:
            return pl.BlockSpec((2, None, tm, w), lambda b, i: (0, b, i, colblock))
        return pl.BlockSpec((None, tm, w), lambda b, i: (b, i, colblock))

    def cmspec(colblock, dirs=False):
        if not colmajor:
            return rmspec(colblock, dirs)
        if dirs:
            return pl.BlockSpec((2, None, GRID_W, rt, w), lambda b, i: (0, b, 0, i, colblock))
        return pl.BlockSpec((None, GRID_W, rt, w), lambda b, i: (b, 0, i, colblock))

    def cm(t):
        return t.reshape(t.shape[:-2] + (GRID_W, rows, t.shape[-1])) if colmajor else t

    s5_spec = pl.BlockSpec((None, None, GRID_W, S5_HALF_W), lambda b, i: (b, i // 2, 0, i % 2)) if colmajor else rmspec(0)
    g2, w_in2, w_out2, g_final = ffn
    final = g_final is not None
    ffn_ins = [modv, modv, modv, g2, w_in2, w_out2] + ([g_final] if final else [])
    ffn_specs = [_mod_spec(6, per_batch), _mod_spec(7, per_batch), _mod_spec(8, per_batch),
                 _const((1, d)), _const((d, 2 * D_FF)), _const((D_FF, d))] + ([_const((1, d))] if final else [])
    return pl.pallas_call(
        functools.partial(_post_kernel, colmajor=colmajor, rt=rt, final=final), grid=(bsz, length // tm),
        in_specs=[pl.BlockSpec((None, tm, d), lambda b, i: (b, i, 0)), _mod_spec(5, per_batch),
                  rmspec(0, True), rmspec(3), cmspec(0, True), cmspec(0), cmspec(0),
                  rmspec(0, True), rmspec(1), s5_spec, rmspec(0),
                  _const((1, w)), _const((1, w)), _const((1, w)), _const((1, w)), _const((1, w)),
                  _const((w, w)), _const((1, w)), _const((w, w)), _const((4 * w, d))] + ffn_specs,
        out_specs=pl.BlockSpec((None, tm, d), lambda b, i: (b, i, 0)),
        out_shape=jax.ShapeDtypeStruct(x.shape, F32),
        compiler_params=_params(("arbitrary", "arbitrary")), name="post_ffn",
    )(x, modv, gla_o, gla_proj, cm(ssd_y), cm(ssd_xbc), cm(ssd_z), ml_h, ml_vo, s5_y, s5_u,
      pw["gla_g"], pw["ssd_d"], pw["ssd_g"], pw["ml_g"], pw["s5_d"], pw["w_glu"], pw["b_glu"], pw["avg"],
      pw["w_out"], *ffn_ins)


def _token_mixing(h_in, modv, g1, lw, states, colmajor):
    bsz, length, _ = h_in.shape
    pr = _inproj(h_in, modv, g1, lw["w_in"], colmajor)
    xbc = _conv_silu(pr["ssd_xbc"], lw["ssd_conv_w"], lw["ssd_conv_b"])
    mqk = _conv_silu(pr["ml_qk"], lw["ml_conv_w"], lw["ml_conv_b"])
    gla_o, gla_s = _gla(pr["gla"], pr["small"], lw["gla_w_lr2"], lw["gla_b_lr2"], states["gla"])
    ssd_y, ssd_s = _ssd(xbc, pr["small_cm"], lw["ssd_a_log"], lw["ssd_dt_bias"], states["ssd"])
    ml_h, ml_c, ml_nm = _mlstm(mqk, pr["ml_vo"], pr["small"], lw["ml_gate_bias"], states["ml_c"], states["ml_nm"])
    uc = pr["s5_chunks"] if colmajor else _s5_rows_from_tokens(pr["s5"])
    y5, s5r, s5i = _s5(uc, lw["s5_mats"], states["s5_r"], states["s5_i"])
    y5 = y5 if colmajor else _s5_rows_to_tokens(y5)
    outs = dict(gla_o=gla_o, gla_proj=pr["gla"], ssd_y=ssd_y, ssd_xbc=xbc, ssd_z=pr["ssd_z"], ml_h=ml_h,
                ml_vo=pr["ml_vo"], s5_y=y5, s5_u=pr["s5"])
    finals = dict(gla=gla_s, ssd=ssd_s, ml_c=ml_c, ml_nm=ml_nm, s5_r=s5r, s5_i=s5i)
    return outs, finals


def _proj_weight(w):
    cols = [w[:, 0:1024], w[:, 1056:1568], w[:, 1568:1824], w[:, 1832:2344], w[:, 2344:2856], w[:, 2872:3128],
            w[:, 1024:1056], w[:, 1824:1832], w[:, 2856:2872], jnp.zeros((w.shape[0], SMALL_W - 56), w.dtype)]
    return jnp.concatenate(cols, axis=1).astype(BF16)


def kernel(x, c, ctx, c_ctx, w_mod, b_mod, g_norm, ffn_w_in, ffn_w_out, w_in, w_out, gla_w_lr2, gla_b_lr2,
           gla_g_norm, ssd_conv_w, ssd_conv_b, ssd_a_log, ssd_dt_bias, ssd_d, ssd_g_norm, ml_conv_w, ml_conv_b,
           ml_gate_bias, ml_g_norm, s5_a_re, s5_a_im, s5_log_step, s5_b_re, s5_b_im, s5_c_re, s5_c_im, s5_d,
           s5_w_glu, s5_b_glu, g_final):
    bsz, length, d = x.shape
    depth = w_mod.shape[0]
    assert bsz + 1 <= MOD_ROWS and d == D_MODEL and length % 1024 == 0 and ctx.shape[1] % SCAN_TB == 0
    cc = jnp.concatenate([c, c_ctx[None, :], jnp.zeros((MOD_ROWS - bsz - 1, d), F32)], axis=0)
    mod = _modulation(cc, w_mod, b_mod)
    avg = jnp.kron(jnp.eye(N_HEADS, dtype=F32), jnp.full((HEAD_DIM, HEAD_DIM), 1.0 / HEAD_DIM, F32)).astype(BF16)
    row = lambda t: t.reshape(1, -1).astype(F32)
    zero_states = dict(
        gla=jnp.zeros((bsz, 2, HW, HW), F32), ssd=jnp.zeros((bsz, 2, HW, HW), F32),
        ml_c=jnp.zeros((bsz, 2, HW, HW), F32), ml_nm=jnp.zeros((bsz, 2, 8, HW), F32),
        s5_r=jnp.zeros((2, 1, bsz * S5_GROUPS * S5_STATE), F32), s5_i=jnp.zeros((2, 1, bsz * S5_GROUPS * S5_STATE), F32))
    for l in range(depth):
        last = l == depth - 1
        mod_x = mod[l, :bsz].reshape(bsz, N_MOD, 1, d)
        mod_c = mod[l, bsz:bsz + 1].reshape(1, N_MOD, 1, d)
        wi = [ffn_w_in[l, j].astype(BF16) for j in range(2)]
        wo = [ffn_w_out[l, j].astype(BF16) for j in range(2)]
        g = [row(g_norm[l, j]) for j in range(3)]
        lw = dict(
            w_in=_proj_weight(w_in[l]), gla_w_lr2=gla_w_lr2[l], gla_b_lr2=gla_b_lr2[l][:, None, :],
            ssd_conv_w=ssd_conv_w[l], ssd_conv_b=row(ssd_conv_b[l]), ssd_a_log=ssd_a_log[l].astype(F32),
            ssd_dt_bias=ssd_dt_bias[l], ml_conv_w=ml_conv_w[l], ml_conv_b=row(ml_conv_b[l]),
            ml_gate_bias=ml_gate_bias[l],
            s5_mats=_s5_prep(s5_a_re[l], s5_a_im[l], s5_log_step[l], s5_b_re[l], s5_b_im[l], s5_c_re[l], s5_c_im[l]))
        pw = dict(gla_g=row(gla_g_norm[l]), ssd_d=row(jnp.repeat(ssd_d[l], HEAD_DIM)), ssd_g=row(ssd_g_norm[l]),
                  ml_g=row(ml_g_norm[l]), s5_d=row(s5_d[l]), w_glu=s5_w_glu[l].astype(BF16), b_glu=row(s5_b_glu[l]),
                  avg=avg, w_out=w_out[l].astype(BF16))
        x = _ffn(x, mod_x, 0, g[0], wi[0], wo[0])
        ctx = _ffn(ctx, mod_c, 0, g[0], wi[0], wo[0])
        outs_c, finals = _token_mixing(ctx, mod_c, g[1], lw, zero_states, colmajor=False)
        outs_x, _ = _token_mixing(x, mod_x, g[1], lw, finals, colmajor=True)
        x = _post(x, mod_x, pw=pw, colmajor=True, ffn=(g[2], wi[1], wo[1], row(g_final) if last else None), **outs_x)
        if not last:
            ctx = _post(ctx, mod_c, pw=pw, colmajor=False, ffn=(g[2], wi[1], wo[1], None), **outs_c)
    return x
```

```python
import functools

import jax
import jax.numpy as jnp
from jax import lax
from jax.experimental import pallas as pl
from jax.experimental.pallas import tpu as pltpu

F32 = jnp.float32
BF16 = jnp.bfloat16
HI = lax.Precision.HIGHEST

D_MODEL = 1024
D_FF = 2816
GRID_W = 64
GROUP_W = 256
HEAD_DIM = 64
N_HEADS = 4
CHUNK = 64
EPS = 1e-6
N_MOD = 9
GLA_RANK = 16
GLA_NORMALISER = 16.0
S5_CH = 16
S5_GROUPS = 16
S5_STATE = 64
S5_T = 16
S5_PAIRS = S5_GROUPS // 2
SMALL_W = 128
PROJ_W = 3072 + SMALL_W
MOD_ROWS = 8
VMEM_LIMIT = 56 * 1024 * 1024


def _params(sem, vmem=VMEM_LIMIT):
    return pltpu.CompilerParams(dimension_semantics=sem, vmem_limit_bytes=vmem)


def _mm(a, b):
    return jnp.dot(a.astype(BF16), b.astype(BF16), preferred_element_type=F32)


def _mm_nt(a, b):
    return lax.dot_general(a.astype(BF16), b.astype(BF16), (((1,), (1,)), ((), ())), preferred_element_type=F32)


def _mm_tn(a, b):
    return lax.dot_general(a.astype(BF16), b.astype(BF16), (((0,), (0,)), ((), ())), preferred_element_type=F32)


def _mm_hi(a, b):
    return jnp.dot(a, b, precision=HI, preferred_element_type=F32)


def _mm_tn_hi(a, b):
    return lax.dot_general(a, b, (((0,), (0,)), ((), ())), precision=HI, preferred_element_type=F32)


def _rms(t):
    return t * lax.rsqrt(jnp.mean(t * t, axis=-1, keepdims=True) + EPS)


def _modnorm(xv, g, shift, scale):
    return (_rms(xv) * g) * (1.0 + scale) + shift


def _const(shape):
    n = len(shape)
    return pl.BlockSpec(shape, lambda *_: (0,) * n, pipeline_mode=pl.Buffered(1))


def _mod_kernel(c_ref, w_ref, b_ref, o_ref):
    o_ref[...] = _mm_hi(jax.nn.silu(c_ref[...]), w_ref[...]) + b_ref[...]


def _modulation(cc, w_mod, b_mod):
    depth, d, n = w_mod.shape
    tn = 1024
    return pl.pallas_call(
        _mod_kernel, grid=(depth, n // tn),
        in_specs=[pl.BlockSpec((MOD_ROWS, d), lambda l, j: (0, 0)),
                  pl.BlockSpec((None, d, tn), lambda l, j: (l, 0, j)),
                  pl.BlockSpec((None, 1, tn), lambda l, j: (l, 0, j))],
        out_specs=pl.BlockSpec((None, MOD_ROWS, tn), lambda l, j: (l, 0, j)),
        out_shape=jax.ShapeDtypeStruct((depth, MOD_ROWS, n), F32),
        compiler_params=_params(("arbitrary", "arbitrary")), name="modulation",
    )(cc, w_mod, b_mod.reshape(depth, 1, n))


def _mod_spec(k, per_batch):
    if per_batch:
        return pl.BlockSpec((None, None, 1, D_MODEL), lambda b, i: (b, k, 0, 0))
    return pl.BlockSpec((None, None, 1, D_MODEL), lambda b, i: (0, k, 0, 0))


def _ffn_kernel(x_ref, sh_ref, sc_ref, gt_ref, g_ref, wi_ref, wo_ref, *rest, final):
    o_ref = rest[-1]
    xv = x_ref[...]
    h = _modnorm(xv, g_ref[...], sh_ref[...], sc_ref[...]).astype(BF16)
    gu = jnp.dot(h, wi_ref[...], preferred_element_type=F32)
    a = (jax.nn.silu(gu[:, :D_FF]) * gu[:, D_FF:]).astype(BF16)
    y = jnp.dot(a, wo_ref[...], preferred_element_type=F32)
    out = xv + (0.5 * gt_ref[...]) * y
    if final:
        out = _rms(out) * rest[0][...]
    o_ref[...] = out


def _ffn(x, modv, j, g, w_in, w_out, lj, g_final=None):
    bsz, length, d = x.shape
    tm = min(512, length)
    per_batch = modv.shape[0] > 1
    final = g_final is not None
    ins = [x, modv, modv, modv, g, w_in, w_out]
    specs = [pl.BlockSpec((None, tm, d), lambda b, i: (b, i, 0)),
             _mod_spec(3 * j, per_batch), _mod_spec(3 * j + 1, per_batch), _mod_spec(3 * j + 2, per_batch),
             _const((1, d)),
             pl.BlockSpec((None, None, d, 2 * D_FF), lambda b, i: lj + (0, 0), pipeline_mode=pl.Buffered(1)),
             pl.BlockSpec((None, None, D_FF, d), lambda b, i: lj + (0, 0), pipeline_mode=pl.Buffered(1))]
    if final:
        ins.append(g_final)
        specs.append(_const((1, d)))
    return pl.pallas_call(
        functools.partial(_ffn_kernel, final=final), grid=(bsz, length // tm),
        in_specs=specs, out_specs=pl.BlockSpec((None, tm, d), lambda b, i: (b, i, 0)),
        out_shape=jax.ShapeDtypeStruct(x.shape, F32),
        compiler_params=_params(("arbitrary", "arbitrary")), name="ffn",
    )(*ins)


_PROJ_SPLIT = (("gla", 0, 1024, False), ("ml_qk", 1792, 2304, False), ("ml_vo", 2304, 2816, False),
               ("small", 3072, 3200, False), ("ssd_xbc", 1024, 1536, True), ("ssd_z", 1536, 1792, True),
               ("s5", 2816, 3072, False), ("small_cm", 3072, 3200, True))
_ML_QK = (1792, 2304)
S5_HALF = S5_T // 2
S5_HALF_W = S5_PAIRS * S5_HALF * 2 * S5_CH


def _to_s5_chunks(ys):
    yt = ys.T
    pw = 2 * S5_CH
    outs = []
    for pp in range(S5_PAIRS):
        rows = yt[pp * pw:(pp + 1) * pw]
        z = jnp.concatenate([rows[:, s2 * 128:(s2 + 1) * 128] for s2 in range(S5_HALF // 2)], axis=0)
        zt = z.T
        outs.append(jnp.concatenate([zt[0:GRID_W], zt[GRID_W:2 * GRID_W]], axis=1))
    return jnp.concatenate(outs, axis=1)


def _from_s5_chunks(yc):
    pw = 2 * S5_CH
    cw = S5_HALF * pw
    rows = []
    for pp in range(S5_PAIRS):
        blk = yc[:, pp * cw:(pp + 1) * cw]
        z = jnp.concatenate([blk[:, 0:cw // 2], blk[:, cw // 2:cw]], axis=0).T
        rows.append(jnp.concatenate([z[s2 * pw:(s2 + 1) * pw] for s2 in range(S5_HALF // 2)], axis=1))
    return jnp.concatenate(rows, axis=0).T


def _conv3_silu(v, prev, nxt, w_ref, b_ref):
    n = v.shape[0]
    rid = lax.broadcasted_iota(jnp.int32, v.shape, 0)
    vm = jnp.where(rid == 0, prev, pltpu.roll(v, 1, 0))
    vp = jnp.where(rid == n - 1, nxt, pltpu.roll(v, n - 1, 0))
    return jax.nn.silu(w_ref[0:1, :] * vm + w_ref[1:2, :] * v + w_ref[2:3, :] * vp + b_ref[...])


def _inproj_kernel(x_ref, xp_ref, xn_ref, sh_ref, sc_ref, g_ref, w_ref, cw_ref, cb_ref, *outs, colmajor, rt):
    i = pl.program_id(1)
    h = _modnorm(x_ref[...], g_ref[...], sh_ref[...], sc_ref[...]).astype(BF16)
    y = jnp.dot(h, w_ref[...], preferred_element_type=F32)
    halo = jnp.concatenate([xp_ref[...], xn_ref[...]], axis=0)
    hh = _modnorm(halo, g_ref[...], sh_ref[...], sc_ref[...]).astype(BF16)
    lo_qk, hi_qk = _ML_QK
    yh = jnp.dot(hh, w_ref[:, lo_qk:hi_qk], preferred_element_type=F32)
    prev = jnp.where(i > 0, yh[7:8, :], 0.0)
    nxt = jnp.where(i < pl.num_programs(1) - 1, yh[8:9, :], 0.0)
    for o_ref, (name, lo, hi, cm) in zip(outs, _PROJ_SPLIT):
        v = y[:, lo:hi]
        if name == "ml_qk":
            v = _conv3_silu(v, prev, nxt, cw_ref, cb_ref)
        if cm and colmajor:
            v = jnp.swapaxes(v.reshape(rt, GRID_W, hi - lo), 0, 1)
        o_ref[...] = v
        if name == "s5" and colmajor:
            outs[-1][...] = _to_s5_chunks(v)


def _inproj(x, modv, g, w, conv_w, conv_b, colmajor):
    bsz, length, d = x.shape
    tm = min(512, length)
    t8, last8 = tm // 8, length // 8 - 1
    rt = tm // GRID_W
    rows = length // GRID_W
    per_batch = modv.shape[0] > 1
    lo_qk, hi_qk = _ML_QK
    shapes, specs = [], []
    for _, lo, hi, cm in _PROJ_SPLIT:
        n = hi - lo
        if cm and colmajor:
            shapes.append(jax.ShapeDtypeStruct((bsz, GRID_W, rows, n), F32))
            specs.append(pl.BlockSpec((None, GRID_W, rt, n), lambda b, i: (b, 0, i, 0)))
        else:
            shapes.append(jax.ShapeDtypeStruct((bsz, length, n), F32))
            specs.append(pl.BlockSpec((None, tm, n), lambda b, i: (b, i, 0)))
    if colmajor:
        assert rt == S5_HALF and rows % S5_T == 0
        shapes.append(jax.ShapeDtypeStruct((bsz, rows // S5_T, GRID_W, 2 * S5_HALF_W), F32))
        specs.append(pl.BlockSpec((None, None, GRID_W, S5_HALF_W), lambda b, i: (b, i // 2, 0, i % 2)))
    outs = pl.pallas_call(
        functools.partial(_inproj_kernel, colmajor=colmajor, rt=rt), grid=(bsz, length // tm),
        in_specs=[pl.BlockSpec((None, tm, d), lambda b, i: (b, i, 0)),
                  pl.BlockSpec((None, 8, d), lambda b, i: (b, jnp.maximum(i * t8 - 1, 0), 0)),
                  pl.BlockSpec((None, 8, d), lambda b, i: (b, jnp.minimum((i + 1) * t8, last8), 0)),
                  _mod_spec(3, per_batch), _mod_spec(4, per_batch), _const((1, d)), _const((d, PROJ_W)),
                  _const((3, hi_qk - lo_qk)), _const((1, hi_qk - lo_qk))],
        out_specs=specs, out_shape=shapes,
        compiler_params=_params(("arbitrary", "arbitrary")), name="inproj",
    )(x, x, x, modv, modv, g, w, conv_w, conv_b)
    res = {}
    for o, (name, lo, hi, cm) in zip(outs, _PROJ_SPLIT):
        res[name] = o.reshape(bsz, length, hi - lo)
    if colmajor:
        res["s5_chunks"] = outs[-1]
    return res


def _conv_kernel(x_ref, p_ref, n_ref, w_ref, b_ref, o_ref, *, tb):
    i = pl.program_id(1)
    nb = pl.num_programs(1)
    prev = jnp.where(i > 0, p_ref[7:8, :], 0.0)
    nxt = jnp.where(i < nb - 1, n_ref[0:1, :], 0.0)
    o_ref[...] = _conv3_silu(x_ref[...], prev, nxt, w_ref, b_ref)


def _conv_silu(x, w, b):
    bsz, length, ch = x.shape
    tb = min(1024, length)
    t8 = tb // 8
    last8 = length // 8 - 1
    return pl.pallas_call(
        functools.partial(_conv_kernel, tb=tb), grid=(bsz, length // tb),
        in_specs=[pl.BlockSpec((None, tb, ch), lambda bb, i: (bb, i, 0)),
                  pl.BlockSpec((None, 8, ch), lambda bb, i: (bb, jnp.maximum(i * t8 - 1, 0), 0)),
                  pl.BlockSpec((None, 8, ch), lambda bb, i: (bb, jnp.minimum((i + 1) * t8, last8), 0)),
                  _const((3, ch)), _const((1, ch))],
        out_specs=pl.BlockSpec((None, tb, ch), lambda bb, i: (bb, i, 0)),
        out_shape=jax.ShapeDtypeStruct(x.shape, F32),
        compiler_params=_params(("arbitrary", "arbitrary")), name="conv_silu",
    )(x, x, x, w, b)


HW = N_HEADS * HEAD_DIM
CPB = 4
SCAN_TB = CPB * CHUNK


def _scan_block(d, i, nb):
    return jnp.where(d == 0, i, nb - 1 - i)


def _chunk_rows(d, cc):
    c = jnp.where(d == 0, cc, CPB - 1 - cc)
    return pl.ds(pl.multiple_of(c * CHUNK, CHUNK), CHUNK)


def _scan_consts(d):
    t = lax.broadcasted_iota(jnp.int32, (CHUNK, HW), 0)
    lane = lax.broadcasted_iota(jnp.int32, (CHUNK, HW), 1)
    s = jnp.bitwise_and(lane, HEAD_DIM - 1)
    vis = jnp.where(d == 0, t - s, s - t)
    r64 = lax.broadcasted_iota(jnp.int32, (CHUNK, CHUNK), 0)
    c64 = lax.broadcasted_iota(jnp.int32, (CHUNK, CHUNK), 1)
    rb = lax.broadcasted_iota(jnp.int32, (HW, HW), 0) // HEAD_DIM
    cb = lax.broadcasted_iota(jnp.int32, (HW, HW), 1) // HEAD_DIM
    return dict(
        mask4=vis >= 0,
        maskt4=(vis <= 0).astype(F32),
        eye4=(vis == 0).astype(F32),
        cumsum=(jnp.where(d == 0, r64 - c64, c64 - r64) >= 0).astype(BF16),
        bd=(rb == cb).astype(BF16),
        head=lane // HEAD_DIM)


def _split(x, terms):
    parts = []
    for _ in range(terms - 1):
        parts.append(x.astype(BF16))
        x = x - parts[-1].astype(F32)
    return parts + [x.astype(BF16)]


def _mm_r01(x, w01, terms=2):
    return sum(jnp.dot(p, w01, preferred_element_type=F32) for p in _split(x, terms))


def _mm_l01(w01, x, terms=3):
    return sum(jnp.dot(w01, p, preferred_element_type=F32) for p in _split(x, terms))


def _bd(x, bd01):
    return jnp.concatenate([x.astype(BF16)] * N_HEADS, axis=0) * bd01


def _scan_specs(nb, bsz):
    def blk(width, colblock):
        return pl.BlockSpec((bsz, SCAN_TB, width), lambda d, i: (0, _scan_block(d, i, nb), colblock))
    out = pl.BlockSpec((None, bsz, SCAN_TB, GROUP_W), lambda d, i: (d, 0, _scan_block(d, i, nb), 0))
    state = pl.BlockSpec((bsz, None, HW, HW), lambda d, i: (0, d, 0, 0))
    return blk, out, state


def _dir_spec(shape):
    n = len(shape)
    return pl.BlockSpec((None,) + shape, lambda d, i: (d,) + (0,) * n)


def _gla_kernel(q_ref, k_ref, v_ref, sm_ref, w_ref, b_ref, s0_ref, o_ref, sf_ref, s_ref, *, bsz):
    d = pl.program_id(0)
    i = pl.program_id(1)

    @pl.when(i == 0)
    def _():
        s_ref[...] = s0_ref[...]

    cs = _scan_consts(d)
    bs = range(bsz)
    for cc in range(CPB):
        rows = _chunk_rows(d, cc)
        sm = sm_ref[:, rows, :].reshape(bsz * CHUNK, SMALL_W)
        lr = jnp.where(d == 0, sm[:, 0:GLA_RANK], sm[:, GLA_RANK:2 * GLA_RANK])
        g = jax.nn.log_sigmoid(_mm(lr, w_ref[...]) + b_ref[...]) * (1.0 / GLA_NORMALISER)
        g = [g[b * CHUNK:(b + 1) * CHUNK] for b in bs]
        bc = _mm_l01(cs["cumsum"], jnp.concatenate(g, axis=1))
        bc = [bc[:, b * HW:(b + 1) * HW] for b in bs]
        bt = [jnp.sum(g[b], axis=0, keepdims=True) for b in bs]
        k = [k_ref[b, rows, :] for b in bs]
        v = [v_ref[b, rows, :].astype(BF16) for b in bs]
        qt = [(q_ref[b, rows, :] * (HEAD_DIM ** -0.5) * jnp.exp(bc[b])).astype(BF16) for b in bs]
        st = [s_ref[b] for b in bs]
        o_st = [_mm_nt(qt[b], st[b].astype(BF16) * cs["bd"]) for b in bs]
        kbd = [_bd(k[b] * jnp.exp(-bc[b]), cs["bd"]) for b in bs]
        att = [jnp.where(cs["mask4"], _mm_nt(qt[b], kbd[b]), 0.0) for b in bs]
        ks = [(k[b] * jnp.exp(bt[b] - bc[b])).astype(BF16) for b in bs]
        upd = [_mm_tn(v[b], ks[b]) for b in bs]
        vbd = [_bd(v[b], cs["bd"]) for b in bs]
        for b in bs:
            o_ref[b, rows, :] = _mm(att[b], vbd[b]) + o_st[b]
        for b in bs:
            s_ref[b] = st[b] * jnp.exp(bt[b]) + upd[b]

    @pl.when(i == pl.num_programs(1) - 1)
    def _():
        sf_ref[...] = s_ref[...]


def _gla(proj, small, w_lr2, b_lr2, s0):
    bsz, length, _ = proj.shape
    nb = length // SCAN_TB
    blk, out, state = _scan_specs(nb, bsz)
    return pl.pallas_call(
        functools.partial(_gla_kernel, bsz=bsz), grid=(2, nb),
        in_specs=[blk(GROUP_W, 0), blk(GROUP_W, 1), blk(GROUP_W, 2), blk(SMALL_W, 0),
                  _dir_spec((GLA_RANK, GROUP_W)), _dir_spec((1, GROUP_W)), state],
        out_specs=[out, state],
        out_shape=[jax.ShapeDtypeStruct((2, bsz, length, GROUP_W), F32), jax.ShapeDtypeStruct(s0.shape, F32)],
        scratch_shapes=[pltpu.VMEM((bsz, HW, HW), F32)],
        compiler_params=_params(("arbitrary",) * 2), name="gla",
    )(proj, proj, proj, small, w_lr2, b_lr2, s0)


def _ssd_kernel(x_ref, b_ref, c_ref, sm_ref, al_ref, db_ref, s0_ref, o_ref, sf_ref, s_ref, *, bsz):
    d = pl.program_id(0)
    i = pl.program_id(1)

    @pl.when(i == 0)
    def _():
        s_ref[...] = s0_ref[...]

    cs = _scan_consts(d)
    r = lax.broadcasted_iota(jnp.int32, (SMALL_W, HW), 0)
    c = lax.broadcasted_iota(jnp.int32, (SMALL_W, HW), 1)
    e_dt = (r == 32 + N_HEADS * d + c // HEAD_DIM).astype(BF16)
    e_grp = ((r // HEAD_DIM == c // (2 * HEAD_DIM)) & (r % HEAD_DIM == c % HEAD_DIM)).astype(BF16)
    rg = lax.broadcasted_iota(jnp.int32, (HW, 2 * HEAD_DIM), 0) // (2 * HEAD_DIM)
    cg = lax.broadcasted_iota(jnp.int32, (HW, 2 * HEAD_DIM), 1) // HEAD_DIM
    grp = (rg == cg).astype(BF16)
    nega = -jnp.exp(al_ref[...])
    bs = range(bsz)
    for cc in range(CPB):
        rows = _chunk_rows(d, cc)
        sm = sm_ref[:, rows, :].reshape(bsz * CHUNK, SMALL_W)
        dt = jax.nn.softplus(_mm_r01(sm, e_dt) + db_ref[...])
        a = dt * nega
        dt = [dt[b * CHUNK:(b + 1) * CHUNK] for b in bs]
        a = [a[b * CHUNK:(b + 1) * CHUNK] for b in bs]
        cum = _mm_l01(cs["cumsum"], jnp.concatenate(a, axis=1))
        cum = [cum[:, b * HW:(b + 1) * HW] for b in bs]
        bm = b_ref[:, rows, :].reshape(bsz * CHUNK, 2 * HEAD_DIM).astype(BF16)
        cm = c_ref[:, rows, :].reshape(bsz * CHUNK, 2 * HEAD_DIM).astype(BF16)
        cmx = _mm(cm, e_grp)
        bmx = _mm(bm, e_grp)
        bm = [bm[b * CHUNK:(b + 1) * CHUNK] for b in bs]
        cm = [cm[b * CHUNK:(b + 1) * CHUNK] for b in bs]
        st = [s_ref[b] for b in bs]
        y_st = [_mm(cmx[b * CHUNK:(b + 1) * CHUNK], st[b].astype(BF16) * cs["bd"]) for b in bs]
        cb4 = [_mm_nt(cm[b], jnp.concatenate([bm[b]] * N_HEADS, axis=0) * grp) for b in bs]
        tot = [jnp.sum(a[b], axis=0, keepdims=True) for b in bs]
        cumr = [jnp.sum(a[b] * cs["maskt4"], axis=0, keepdims=True) for b in bs]
        xdt = [(x_ref[b, rows, :] * dt[b]).astype(BF16) for b in bs]
        bw = [(bmx[b * CHUNK:(b + 1) * CHUNK] * jnp.exp(tot[b] - cum[b])).astype(BF16) for b in bs]
        upd = [_mm_tn(bw[b], xdt[b]) for b in bs]
        xbd = [_bd(xdt[b], cs["bd"]) for b in bs]
        sc = [cb4[b] * jnp.exp(jnp.where(cs["mask4"], cum[b] - cumr[b], -jnp.inf)) for b in bs]
        for b in bs:
            o_ref[b, rows, :] = _mm(sc[b], xbd[b]) + jnp.exp(cum[b]) * y_st[b]
        for b in bs:
            s_ref[b] = st[b] * jnp.exp(tot[b]) + upd[b]

    @pl.when(i == pl.num_programs(1) - 1)
    def _():
        sf_ref[...] = s_ref[...]


def _per_head_lanes(t):
    return jnp.repeat(t.astype(F32), HEAD_DIM, axis=-1)[:, None, :]


def _ssd(xbc, small, a_log, dt_bias, s0):
    bsz, length, _ = xbc.shape
    nb = length // SCAN_TB
    blk, out, state = _scan_specs(nb, bsz)
    return pl.pallas_call(
        functools.partial(_ssd_kernel, bsz=bsz), grid=(2, nb),
        in_specs=[blk(GROUP_W, 0), blk(128, 2), blk(128, 3), blk(SMALL_W, 0),
                  _dir_spec((1, HW)), _dir_spec((1, HW)), state],
        out_specs=[out, state],
        out_shape=[jax.ShapeDtypeStruct((2, bsz, length, GROUP_W), F32), jax.ShapeDtypeStruct(s0.shape, F32)],
        scratch_shapes=[pltpu.VMEM((bsz, HW, HW), F32)],
        compiler_params=_params(("arbitrary",) * 2), name="ssd",
    )(xbc, xbc, xbc, small, _per_head_lanes(a_log), _per_head_lanes(dt_bias), s0)


def _mlstm_kernel(q_ref, k_ref, v_ref, sm_ref, gi_ref, gf_ref, c0_ref, nm0_ref, o_ref, cf_ref, nmf_ref,
                  c_ref, nm_ref, *, bsz):
    d = pl.program_id(0)
    i = pl.program_id(1)

    @pl.when(i == 0)
    def _():
        c_ref[...] = c0_ref[...]
        nm_ref[...] = nm0_ref[...]

    cs = _scan_consts(d)
    r = lax.broadcasted_iota(jnp.int32, (SMALL_W, 2 * HW), 0)
    c = lax.broadcasted_iota(jnp.int32, (SMALL_W, 2 * HW), 1)
    gate = 40 + 2 * N_HEADS * d + N_HEADS * (c // HW) + (c % HW) // HEAD_DIM
    e_if = (r == gate).astype(BF16)
    rid = lax.broadcasted_iota(jnp.int32, (CHUNK, 1), 0)
    last = rid == jnp.where(d == 0, CHUNK - 1, 0)
    bs = range(bsz)

    def per_b(t):
        return [t[b * CHUNK:(b + 1) * CHUNK] for b in bs]

    for cc in range(CPB):
        rows = _chunk_rows(d, cc)
        sm = sm_ref[:, rows, :].reshape(bsz * CHUNK, SMALL_W)
        gates = _mm_r01(sm, e_if)
        gi = per_b(gates[:, :HW] + gi_ref[...])
        lf = per_b(jax.nn.log_sigmoid(gates[:, HW:] + gf_ref[...]))
        fc = _mm_l01(cs["cumsum"], jnp.concatenate(lf, axis=1))
        fc = [fc[:, b * HW:(b + 1) * HW] for b in bs]
        q = [q_ref[b, rows, :] for b in bs]
        qb = [q[b].astype(BF16) for b in bs]
        k = [k_ref[b, rows, :] * (HEAD_DIM ** -0.5) for b in bs]
        v = [v_ref[b, rows, :].astype(BF16) for b in bs]
        cst = [c_ref[b] for b in bs]
        ns = [nm_ref[b, 0:1, :] for b in bs]
        ms = [nm_ref[b, 1:2, :] for b in bs]
        n_st = [_mm(qb[b], cst[b].astype(BF16) * cs["bd"]) for b in bs]
        qn = [_split(q[b] * ns[b], 2) for b in bs]
        qk = [_mm_nt(qb[b], _bd(k[b], cs["bd"])) for b in bs]
        ftot = [jnp.sum(lf[b], axis=0, keepdims=True) for b in bs]
        fr = [jnp.sum(lf[b] * cs["maskt4"], axis=0, keepdims=True) for b in bs]
        ir = [jnp.sum(gi[b] * cs["eye4"], axis=0, keepdims=True) for b in bs]
        li = [jnp.where(cs["mask4"], fc[b] - fr[b] + ir[b], -jnp.inf) for b in bs]
        m = []
        for b in bs:
            rmax = jnp.zeros_like(li[b])
            for h in range(N_HEADS):
                mh = jnp.max(li[b][:, h * HEAD_DIM:(h + 1) * HEAD_DIM], axis=-1, keepdims=True)
                rmax = jnp.where(cs["head"] == h, mh, rmax)
            m.append(jnp.maximum(fc[b] + ms[b], rmax))
        m_new = [jnp.sum(jnp.where(last, m[b], 0.0), axis=0, keepdims=True) for b in bs]
        kw = [(k[b] * jnp.exp(ftot[b] - fc[b] + gi[b] - m_new[b])) for b in bs]
        upd = [_mm_tn(kw[b], v[b]) for b in bs]
        wf = [qk[b] * jnp.exp(li[b] - m[b]) for b in bs]
        w = [wf[b].astype(BF16) for b in bs]
        w_lo = [(wf[b] - w[b].astype(F32)).astype(BF16) for b in bs]
        vbd = [_bd(v[b], cs["bd"]) for b in bs]
        n_in = [_mm(w[b], vbd[b]) for b in bs]
        sums = [jnp.dot(jnp.concatenate(qn[b] + [w[b], w_lo[b]], axis=0), cs["bd"], preferred_element_type=F32)
                for b in bs]
        for b in bs:
            winter = jnp.exp(fc[b] + ms[b] - m[b])
            d_st = sums[b][0:CHUNK] + sums[b][CHUNK:2 * CHUNK]
            d_in = sums[b][2 * CHUNK:3 * CHUNK] + sums[b][3 * CHUNK:]
            den = d_in + winter * d_st
            o_ref[b, rows, :] = (n_in[b] + winter * n_st[b]) / jnp.maximum(jnp.abs(den), jnp.exp(-m[b]))
        for b in bs:
            decay = jnp.exp(ftot[b] + ms[b] - m_new[b])
            c_ref[b] = cst[b] * decay + upd[b]
            nm_ref[b, 0:1, :] = decay * ns[b] + jnp.sum(kw[b], axis=0, keepdims=True)
            nm_ref[b, 1:2, :] = m_new[b]

    @pl.when(i == pl.num_programs(1) - 1)
    def _():
        cf_ref[...] = c_ref[...]
        nmf_ref[...] = nm_ref[...]


def _mlstm(qk, vo, small, gate_bias, c0, nm0):
    bsz, length, _ = qk.shape
    nb = length // SCAN_TB
    blk, out, state = _scan_specs(nb, bsz)
    nm_spec = pl.BlockSpec((bsz, None, 8, HW), lambda d, i: (0, d, 0, 0))
    return pl.pallas_call(
        functools.partial(_mlstm_kernel, bsz=bsz), grid=(2, nb),
        in_specs=[blk(GROUP_W, 0), blk(GROUP_W, 1), blk(GROUP_W, 0), blk(SMALL_W, 0),
                  _dir_spec((1, HW)), _dir_spec((1, HW)), state, nm_spec],
        out_specs=[out, state, nm_spec],
        out_shape=[jax.ShapeDtypeStruct((2, bsz, length, GROUP_W), F32), jax.ShapeDtypeStruct(c0.shape, F32),
                   jax.ShapeDtypeStruct(nm0.shape, F32)],
        scratch_shapes=[pltpu.VMEM((bsz, HW, HW), F32), pltpu.VMEM((bsz, 8, HW), F32)],
        compiler_params=_params(("arbitrary",) * 2), name="mlstm",
    )(qk, qk, vo, small, _per_head_lanes(gate_bias[:, 0]), _per_head_lanes(gate_bias[:, 1]), c0, nm0)


def _s5_slot(s, gl):
    return (s // S5_HALF) * (S5_HALF * 2 * S5_CH) + (s % 2) * 128 + ((s % S5_HALF) // 2) * 2 * S5_CH + gl * S5_CH


def _s5_prep_kernel(arc, aic, arr, air, lst, brn, bin_, bri, bii, ctr, cti, m_ref, q_ref, p_ref, a16r_ref, a16i_ref):
    n, t16 = S5_STATE, S5_T
    gw = S5_T * S5_CH
    lane_t = lax.broadcasted_iota(jnp.int32, (n, gw), 1) // S5_CH
    sel = (lax.broadcasted_iota(jnp.int32, (S5_CH, gw), 1) % S5_CH
           == lax.broadcasted_iota(jnp.int32, (S5_CH, gw), 0)).astype(F32)
    lane_blk = lax.broadcasted_iota(jnp.int32, (S5_CH, gw), 1) // S5_CH
    lane128 = lax.broadcasted_iota(jnp.int32, (S5_CH, 2 * n), 1)
    m_ref[...] = jnp.zeros_like(m_ref)
    q_ref[...] = jnp.zeros_like(q_ref)
    p_ref[...] = jnp.zeros_like(p_ref)

    def discretise(a_re, a_im, step):
        a_re = jnp.minimum(a_re, -1e-4)
        mag = jnp.exp(a_re * step)
        ab_re, ab_im = mag * jnp.cos(a_im * step), mag * jnp.sin(a_im * step)
        den = a_re * a_re + a_im * a_im
        nr, ni = ab_re - 1.0, ab_im
        return ab_re, ab_im, (nr * a_re + ni * a_im) / den, (ni * a_re - nr * a_im) / den

    def powers(ab_re, ab_im):
        pr, pi = [jnp.ones_like(ab_re)], [jnp.zeros_like(ab_re)]
        for _ in range(t16):
            pr.append(pr[-1] * ab_re - pi[-1] * ab_im)
            pi.append(pr[-2] * ab_im + pi[-1] * ab_re)
        return pr, pi

    for d in range(2):
        step_r = jnp.exp(jnp.concatenate([jnp.broadcast_to(lst[d, gl], (1, n)) for gl in range(2)], axis=1))
        ab_re, ab_im, cf_re, cf_im = discretise(arr[d], air[d], step_r)
        prr, pir = powers(ab_re, ab_im)
        a16r_ref[d] = prr[t16]
        a16i_ref[d] = pir[t16]
        bt_re = cf_re * bri[...] - cf_im * bii[...]
        bt_im = cf_re * bii[...] + cf_im * bri[...]
        for s in range(t16):
            e = t16 - 1 - s if d == 0 else s
            qre = bt_re * prr[e] - bt_im * pir[e]
            qim = bt_re * pir[e] + bt_im * prr[e]
            for gl in range(2):
                keep = (lane128 // n) == gl
                r0 = _s5_slot(s, gl)
                q_ref[d, r0:r0 + S5_CH, 0:2 * n] = jnp.where(keep, qre, 0.0).astype(q_ref.dtype)
                q_ref[d, r0:r0 + S5_CH, 2 * n:4 * n] = jnp.where(keep, qim, 0.0).astype(q_ref.dtype)
        for gl in range(2):
            ab_re, ab_im, cf_re, cf_im = discretise(arc[d, gl], aic[d, gl], jnp.exp(lst[d, gl]))
            prc, pic = powers(ab_re, ab_im)
            bb_re = cf_re * brn[gl] - cf_im * bin_[gl]
            bb_im = cf_re * bin_[gl] + cf_im * brn[gl]
            ct_re = _mm_hi(ctr[gl], sel)
            ct_im = _mm_hi(cti[gl], sel)

            def response(exps):
                p_re = jnp.zeros((n, gw), F32)
                p_im = jnp.zeros((n, gw), F32)
                for t in range(t16):
                    p_re = jnp.where(lane_t == t, prc[exps[t]], p_re)
                    p_im = jnp.where(lane_t == t, pic[exps[t]], p_im)
                return ct_re * p_re - ct_im * p_im, ct_re * p_im + ct_im * p_re

            e_re, e_im = response([t if d == 0 else t16 - 1 - t for t in range(t16)])
            r0v = _mm_tn_hi(bb_re, e_re) - _mm_tn_hi(bb_im, e_im)
            for s in range(t16):
                if d == 0:
                    blk = jnp.where(lane_blk >= s, pltpu.roll(r0v, (S5_CH * s) % gw, 1), 0.0)
                else:
                    blk = jnp.where(lane_blk <= s, pltpu.roll(r0v, (gw - S5_CH * (t16 - 1 - s)) % gw, 1), 0.0)
                r0 = _s5_slot(s, gl)
                m_ref[d, r0:r0 + S5_CH, gl * gw:(gl + 1) * gw] = blk.astype(m_ref.dtype)
            c_re, c_im = response([t + 1 if d == 0 else t16 - t for t in range(t16)])
            p_ref[d, gl * n:(gl + 1) * n, gl * gw:(gl + 1) * gw] = c_re.astype(p_ref.dtype)
            p_ref[d, 2 * n + gl * n:2 * n + (gl + 1) * n, gl * gw:(gl + 1) * gw] = (-c_im).astype(p_ref.dtype)
    c = lax.broadcasted_iota(jnp.int32, (2 * gw, 2 * gw), 1)
    half = S5_HALF * 2 * S5_CH
    t_of_c = S5_HALF * (c // half) + 2 * ((c % 128) // (2 * S5_CH)) + (c % half) // 128
    src = ((c % (2 * S5_CH)) // S5_CH) * gw + t_of_c * S5_CH + c % S5_CH
    perm = (lax.broadcasted_iota(jnp.int32, (2 * gw, 2 * gw), 0) == src).astype(BF16)
    for d in range(2):
        m_ref[d] = jnp.dot(m_ref[d], perm, preferred_element_type=F32).astype(m_ref.dtype)
        p_ref[d] = jnp.dot(p_ref[d], perm, preferred_element_type=F32).astype(p_ref.dtype)


def _s5_prep(a_re, a_im, log_step, b_re, b_im, c_re, c_im):
    n, g, ch, pr = S5_STATE, S5_GROUPS, S5_CH, S5_PAIRS
    f = lambda t: t.astype(F32)
    arc, aic = f(a_re)[..., None], f(a_im)[..., None]
    arr, air = f(a_re).reshape(2, pr, 1, 2 * n), f(a_im).reshape(2, pr, 1, 2 * n)
    lst = f(log_step).reshape(2, g, 1, 1)
    pair_rows = lambda t: f(t).reshape(pr, 2, n, ch).transpose(0, 3, 1, 2).reshape(pr, ch, 2 * n)
    ctr, cti = f(c_re).transpose(0, 2, 1), f(c_im).transpose(0, 2, 1)
    gw2 = 2 * S5_T * S5_CH
    spec_c = pl.BlockSpec((2, 2, n, 1), lambda p: (0, p, 0, 0))
    spec_r = pl.BlockSpec((2, None, 1, 2 * n), lambda p: (0, p, 0, 0))
    spec_g = pl.BlockSpec((2, n, ch), lambda p: (p, 0, 0))
    spec_p = pl.BlockSpec((None, ch, 2 * n), lambda p: (p, 0, 0))
    outs = pl.pallas_call(
        _s5_prep_kernel, grid=(pr,),
        in_specs=[spec_c, spec_c, spec_r, spec_r, pl.BlockSpec((2, 2, 1, 1), lambda p: (0, p, 0, 0)),
                  spec_g, spec_g, spec_p, spec_p, spec_g, spec_g],
        out_specs=[pl.BlockSpec((2, None, gw2, gw2), lambda p: (0, p, 0, 0)),
                   pl.BlockSpec((2, None, gw2, 4 * n), lambda p: (0, p, 0, 0)),
                   pl.BlockSpec((2, None, 4 * n, gw2), lambda p: (0, p, 0, 0)),
                   pl.BlockSpec((2, None, 1, 2 * n), lambda p: (0, p, 0, 0)),
                   pl.BlockSpec((2, None, 1, 2 * n), lambda p: (0, p, 0, 0))],
        out_shape=[jax.ShapeDtypeStruct((2, pr, gw2, gw2), BF16), jax.ShapeDtypeStruct((2, pr, gw2, 4 * n), BF16),
                   jax.ShapeDtypeStruct((2, pr, 4 * n, gw2), BF16),
                   jax.ShapeDtypeStruct((2, pr, 1, 2 * n), F32), jax.ShapeDtypeStruct((2, pr, 1, 2 * n), F32)],
        compiler_params=_params(("arbitrary",)), name="s5_prep",
    )(arc, aic, arr, air, lst, f(b_re), f(b_im), pair_rows(b_re), pair_rows(b_im), ctr, cti)
    m, q, p, a16r, a16i = outs
    return m, q, p, a16r.reshape(2, 1, g * n), a16i.reshape(2, 1, g * n)


def _s5_state_kernel(u0_ref, u1_ref, q_ref, vr_ref, vi_ref):
    half = q_ref.shape[0] // 2
    v = (jnp.dot(u0_ref[...], q_ref[0:half, :], preferred_element_type=F32)
         + jnp.dot(u1_ref[...], q_ref[half:, :], preferred_element_type=F32))
    vr_ref[...] = v[:, 0:128]
    vi_ref[...] = v[:, 128:256]


def _s5_scan_kernel(vr_ref, vi_ref, ar_ref, ai_ref, h0r_ref, h0i_ref, hr_ref, hi_ref, fr_ref, fi_ref, *, nch, bsz):
    d = pl.program_id(0)
    ar = jnp.broadcast_to(ar_ref[...], h0r_ref.shape)
    ai = jnp.broadcast_to(ai_ref[...], h0r_ref.shape)
    cpt = 8 // bsz
    ntile = nch // cpt

    def run(reverse):
        def body(kk, carry):
            hr, hi = carry
            t = ntile - 1 - kk if reverse else kk
            rows = pl.ds(pl.multiple_of(t * 8, 8), 8)
            vr8, vi8 = vr_ref[rows, :], vi_ref[rows, :]
            in_r, in_i = [None] * cpt, [None] * cpt
            for j in (range(cpt - 1, -1, -1) if reverse else range(cpt)):
                in_r[j], in_i[j] = hr, hi
                sl = slice(j * bsz, (j + 1) * bsz)
                hr, hi = ar * hr - ai * hi + vr8[sl], ar * hi + ai * hr + vi8[sl]
            hr_ref[rows, :] = jnp.concatenate(in_r, axis=0)
            hi_ref[rows, :] = jnp.concatenate(in_i, axis=0)
            return hr, hi

        hr, hi = lax.fori_loop(0, ntile, body, (h0r_ref[...], h0i_ref[...]))
        fr_ref[...] = hr
        fi_ref[...] = hi

    @pl.when(d == 0)
    def _():
        run(False)

    @pl.when(d == 1)
    def _():
        run(True)


def _s5_out_kernel(u0_ref, u1_ref, m_ref, p_ref, hr_ref, hi_ref, y_ref):
    u = jnp.concatenate([u0_ref[...], u1_ref[...]], axis=-1)
    y = None
    for d in range(2):
        hcat = jnp.concatenate([hr_ref[d], hi_ref[d]], axis=-1).astype(BF16)
        t = jnp.dot(u, m_ref[d], preferred_element_type=F32) + jnp.dot(hcat, p_ref[d], preferred_element_type=F32)
        y = t if y is None else y + t
    y_ref[...] = y


def _s5(ug, mats, h0r, h0i, bsz):
    m, q, p, a16r, a16i = mats
    rows = ug.shape[0]
    nch = rows // bsz
    gn = S5_GROUPS * S5_STATE
    gw2 = 2 * S5_T * S5_CH
    hw = gw2 // 2
    vr, vi = pl.pallas_call(
        _s5_state_kernel, grid=(2, S5_PAIRS),
        in_specs=[pl.BlockSpec((rows, hw), lambda d, pp: (0, pp)),
                  pl.BlockSpec((rows, hw), lambda d, pp: (0, S5_PAIRS + pp)),
                  pl.BlockSpec((None, None, gw2, 256), lambda d, pp: (d, pp, 0, 0))],
        out_specs=[pl.BlockSpec((None, rows, 128), lambda d, pp: (d, 0, pp))] * 2,
        out_shape=[jax.ShapeDtypeStruct((2, rows, gn), F32)] * 2,
        compiler_params=_params(("arbitrary", "arbitrary")), name="s5_state",
    )(ug, ug, q)
    lb = 256
    blk = pl.BlockSpec((None, rows, lb), lambda d, j: (d, 0, j))
    vec = pl.BlockSpec((None, 1, lb), lambda d, j: (d, 0, j))
    st = pl.BlockSpec((None, bsz, lb), lambda d, j: (d, 0, j))
    hr, hi, fr, fi = pl.pallas_call(
        functools.partial(_s5_scan_kernel, nch=nch, bsz=bsz), grid=(2, gn // lb),
        in_specs=[blk, blk, vec, vec, st, st], out_specs=[blk, blk, st, st],
        out_shape=[jax.ShapeDtypeStruct((2, rows, gn), F32)] * 2 + [jax.ShapeDtypeStruct((2, bsz, gn), F32)] * 2,
        compiler_params=_params(("arbitrary", "arbitrary")), name="s5_scan",
    )(vr, vi, a16r, a16i, h0r, h0i)
    y = pl.pallas_call(
        _s5_out_kernel, grid=(S5_PAIRS, 2),
        in_specs=[pl.BlockSpec((rows, hw), lambda pp, hf: (0, pp)),
                  pl.BlockSpec((rows, hw), lambda pp, hf: (0, S5_PAIRS + pp)),
                  pl.BlockSpec((2, None, gw2, hw), lambda pp, hf: (0, pp, 0, hf)),
                  pl.BlockSpec((2, None, 256, hw), lambda pp, hf: (0, pp, 0, hf)),
                  pl.BlockSpec((2, rows, 128), lambda pp, hf: (0, 0, pp)),
                  pl.BlockSpec((2, rows, 128), lambda pp, hf: (0, 0, pp))],
        out_specs=pl.BlockSpec((rows, hw), lambda pp, hf: (0, hf * S5_PAIRS + pp)),
        out_shape=jax.ShapeDtypeStruct((rows, 2 * S5_HALF_W), F32),
        compiler_params=_params(("arbitrary", "arbitrary")), name="s5_out",
    )(ug, ug, m, p, hr, hi)
    return y, fr, fi


def _s5_rows_from_grid(uc):
    bsz, r, w, lanes = uc.shape
    return uc.transpose(2, 1, 0, 3).reshape(w * r * bsz, lanes).astype(BF16)


def _s5_rows_to_grid(y, bsz):
    r = y.shape[0] // (GRID_W * bsz)
    return y.reshape(GRID_W, r, bsz, y.shape[1]).transpose(2, 1, 0, 3)


def _s5_rows_from_tokens(u):
    bsz, length, _ = u.shape
    nch = length // S5_T
    t = u.reshape(bsz, nch, 2, S5_HALF // 2, 2, S5_PAIRS, 2, S5_CH)
    t = t.transpose(1, 0, 2, 5, 4, 3, 6, 7)
    return t.reshape(nch * bsz, 2 * S5_HALF_W).astype(BF16)


def _s5_rows_to_tokens(y, bsz):
    nch = y.shape[0] // bsz
    t = y.reshape(nch, bsz, 2, S5_PAIRS, 2, S5_HALF // 2, 2, S5_CH)
    t = t.transpose(1, 0, 2, 5, 4, 3, 6, 7)
    return t.reshape(bsz, nch * S5_T, S5_GROUPS * S5_CH)


def _post_kernel(x_ref, gt_ref, go_ref, gr_ref, sy_ref, sx_ref, sz_ref, mh_ref, mo_ref, y5_ref, u5_ref,
                 gg_ref, sd_ref, sg_ref, mg_ref, d5_ref, wg_ref, bg_ref, avg_ref, wo_ref, o_ref, *, colmajor, rt):
    def rm(v):
        if not colmajor:
            return v
        return jnp.swapaxes(v, 0, 1).reshape(rt * GRID_W, v.shape[-1])

    def head_rms(t):
        return t * lax.rsqrt(_mm_r01(t * t, avg_ref[...], terms=3) + EPS)

    a = head_rms(go_ref[0] + go_ref[1]) * gg_ref[...] * jax.nn.silu(gr_ref[...])
    ys = rm(sy_ref[0] + sy_ref[1] + sd_ref[...] * sx_ref[...])
    b = _rms(ys * jax.nn.silu(rm(sz_ref[...]))) * sg_ref[...]
    m = head_rms(mh_ref[0] + mh_ref[1]) * mg_ref[...] * jax.nn.sigmoid(mo_ref[...])
    y5 = (_from_s5_chunks(y5_ref[...]) if colmajor else y5_ref[...]) + d5_ref[...] * u5_ref[...]
    gl = jax.nn.gelu(y5)
    dd = gl * jax.nn.sigmoid(_mm(gl, wg_ref[...]) + bg_ref[...])
    cat = jnp.concatenate([a, b, m, dd], axis=-1).astype(BF16)
    o_ref[...] = x_ref[...] + gt_ref[...] * jnp.dot(cat, wo_ref[...], preferred_element_type=F32)


def _post(x, modv, gla_o, gla_proj, ssd_y, ssd_xbc, ssd_z, ml_h, ml_vo, s5_y, s5_u, pw, colmajor):
    bsz, length, d = x.shape
    tm = min(512, length)
    rt = tm // GRID_W
    rows = length // GRID_W
    per_batch = modv.shape[0] > 1
    w = GROUP_W

    def rmspec(colblock, dirs=False):
        if dirs:
            return pl.BlockSpec((2, None, tm, w), lambda b, i: (0, b, i, colblock))
        return pl.BlockSpec((None, tm, w), lambda b, i: (b, i, colblock))

    def cmspec(colblock, dirs=False):
        if not colmajor:
            return rmspec(colblock, dirs)
        if dirs:
            return pl.BlockSpec((2, None, GRID_W, rt, w), lambda b, i: (0, b, 0, i, colblock))
        return pl.BlockSpec((None, GRID_W, rt, w), lambda b, i: (b, 0, i, colblock))

    def cm(t):
        return t.reshape(t.shape[:-2] + (GRID_W, rows, t.shape[-1])) if colmajor else t

    s5_spec = pl.BlockSpec((None, None, GRID_W, S5_HALF_W), lambda b, i: (b, i // 2, 0, i % 2)) if colmajor else rmspec(0)
    return pl.pallas_call(
        functools.partial(_post_kernel, colmajor=colmajor, rt=rt), grid=(bsz, length // tm),
        in_specs=[pl.BlockSpec((None, tm, d), lambda b, i: (b, i, 0)), _mod_spec(5, per_batch),
                  rmspec(0, True), rmspec(3), cmspec(0, True), cmspec(0), cmspec(0),
                  rmspec(0, True), rmspec(1), s5_spec, rmspec(0),
                  _const((1, w)), _const((1, w)), _const((1, w)), _const((1, w)), _const((1, w)),
                  _const((w, w)), _const((1, w)), _const((w, w)), _const((4 * w, d))],
        out_specs=pl.BlockSpec((None, tm, d), lambda b, i: (b, i, 0)),
        out_shape=jax.ShapeDtypeStruct(x.shape, F32),
        compiler_params=_params(("arbitrary", "arbitrary")), name="post",
    )(x, modv, gla_o, gla_proj, cm(ssd_y), cm(ssd_xbc), cm(ssd_z), ml_h, ml_vo, s5_y, s5_u,
      pw["gla_g"], pw["ssd_d"], pw["ssd_g"], pw["ml_g"], pw["s5_d"], pw["w_glu"], pw["b_glu"], pw["avg"],
      pw["w_out"])


def _token_mixing(h_in, modv, g1, lw, states, colmajor):
    bsz, length, _ = h_in.shape
    pr = _inproj(h_in, modv, g1, lw["w_in"], lw["ml_conv_w"], lw["ml_conv_b"], colmajor)
    xbc = _conv_silu(pr["ssd_xbc"], lw["ssd_conv_w"], lw["ssd_conv_b"])
    mqk = pr["ml_qk"]
    gla_o, gla_s = _gla(pr["gla"], pr["small"], lw["gla_w_lr2"], lw["gla_b_lr2"], states["gla"])
    ssd_y, ssd_s = _ssd(xbc, pr["small_cm"], lw["ssd_a_log"], lw["ssd_dt_bias"], states["ssd"])
    ml_h, ml_c, ml_nm = _mlstm(mqk, pr["ml_vo"], pr["small"], lw["ml_gate_bias"], states["ml_c"], states["ml_nm"])
    ug = _s5_rows_from_grid(pr["s5_chunks"]) if colmajor else _s5_rows_from_tokens(pr["s5"])
    y5, s5r, s5i = _s5(ug, lw["s5_mats"], states["s5_r"], states["s5_i"], bsz)
    y5 = _s5_rows_to_grid(y5, bsz) if colmajor else _s5_rows_to_tokens(y5, bsz)
    outs = dict(gla_o=gla_o, gla_proj=pr["gla"], ssd_y=ssd_y, ssd_xbc=xbc, ssd_z=pr["ssd_z"], ml_h=ml_h,
                ml_vo=pr["ml_vo"], s5_y=y5, s5_u=pr["s5"])
    finals = dict(gla=gla_s, ssd=ssd_s, ml_c=ml_c, ml_nm=ml_nm, s5_r=s5r, s5_i=s5i)
    return outs, finals


def _proj_weight(w):
    cols = [w[:, 0:1024], w[:, 1056:1568], w[:, 1568:1824], w[:, 1832:2344], w[:, 2344:2856], w[:, 2872:3128],
            w[:, 1024:1056], w[:, 1824:1832], w[:, 2856:2872], jnp.zeros((w.shape[0], SMALL_W - 56), w.dtype)]
    return jnp.concatenate(cols, axis=1).astype(BF16)


def kernel(x, c, ctx, c_ctx, w_mod, b_mod, g_norm, ffn_w_in, ffn_w_out, w_in, w_out, gla_w_lr2, gla_b_lr2,
           gla_g_norm, ssd_conv_w, ssd_conv_b, ssd_a_log, ssd_dt_bias, ssd_d, ssd_g_norm, ml_conv_w, ml_conv_b,
           ml_gate_bias, ml_g_norm, s5_a_re, s5_a_im, s5_log_step, s5_b_re, s5_b_im, s5_c_re, s5_c_im, s5_d,
           s5_w_glu, s5_b_glu, g_final):
    bsz, length, d = x.shape
    depth = w_mod.shape[0]
    assert bsz + 1 <= MOD_ROWS and d == D_MODEL and length % 1024 == 0 and ctx.shape[1] % SCAN_TB == 0
    cc = jnp.concatenate([c, c_ctx[None, :], jnp.zeros((MOD_ROWS - bsz - 1, d), F32)], axis=0)
    mod = _modulation(cc, w_mod, b_mod)
    avg = jnp.kron(jnp.eye(N_HEADS, dtype=F32), jnp.full((HEAD_DIM, HEAD_DIM), 1.0 / HEAD_DIM, F32)).astype(BF16)
    row = lambda t: t.reshape(1, -1).astype(F32)
    wi, wo = ffn_w_in.astype(BF16), ffn_w_out.astype(BF16)
    zero_states = dict(
        gla=jnp.zeros((bsz, 2, HW, HW), F32), ssd=jnp.zeros((bsz, 2, HW, HW), F32),
        ml_c=jnp.zeros((bsz, 2, HW, HW), F32), ml_nm=jnp.zeros((bsz, 2, 8, HW), F32),
        s5_r=jnp.zeros((2, bsz, S5_GROUPS * S5_STATE), F32), s5_i=jnp.zeros((2, bsz, S5_GROUPS * S5_STATE), F32))
    for l in range(depth):
        last = l == depth - 1
        mod_x = mod[l, :bsz].reshape(bsz, N_MOD, 1, d)
        mod_c = mod[l, bsz:bsz + 1].reshape(1, N_MOD, 1, d)
        g = [row(g_norm[l, j]) for j in range(3)]
        lw = dict(
            w_in=_proj_weight(w_in[l]), gla_w_lr2=gla_w_lr2[l], gla_b_lr2=gla_b_lr2[l][:, None, :],
            ssd_conv_w=ssd_conv_w[l], ssd_conv_b=row(ssd_conv_b[l]), ssd_a_log=ssd_a_log[l].astype(F32),
            ssd_dt_bias=ssd_dt_bias[l], ml_conv_w=ml_conv_w[l], ml_conv_b=row(ml_conv_b[l]),
            ml_gate_bias=ml_gate_bias[l],
            s5_mats=_s5_prep(s5_a_re[l], s5_a_im[l], s5_log_step[l], s5_b_re[l], s5_b_im[l], s5_c_re[l], s5_c_im[l]))
        pw = dict(gla_g=row(gla_g_norm[l]), ssd_d=row(jnp.repeat(ssd_d[l], HEAD_DIM)), ssd_g=row(ssd_g_norm[l]),
                  ml_g=row(ml_g_norm[l]), s5_d=row(s5_d[l]), w_glu=s5_w_glu[l].astype(BF16), b_glu=row(s5_b_glu[l]),
                  avg=avg, w_out=w_out[l].astype(BF16))
        x = _ffn(x, mod_x, 0, g[0], wi, wo, (l, 0))
        ctx = _ffn(ctx, mod_c, 0, g[0], wi, wo, (l, 0))
        outs_c, finals = _token_mixing(ctx, mod_c, g[1], lw, zero_states, colmajor=False)
        outs_x, _ = _token_mixing(x, mod_x, g[1], lw, finals, colmajor=True)
        x = _post(x, mod_x, pw=pw, colmajor=True, **outs_x)
        x = _ffn(x, mod_x, 2, g[2], wi, wo, (l, 1), g_final=row(g_final) if last else None)
        if not last:
            ctx = _post(ctx, mod_c, pw=pw, colmajor=False, **outs_c)
            ctx = _ffn(ctx, mod_c, 2, g[2], wi, wo, (l, 1))
    return x
```

```python
import functools

import jax
import jax.numpy as jnp
from jax import lax
from jax.experimental import pallas as pl
from jax.experimental.pallas import tpu as pltpu

F32 = jnp.float32
BF16 = jnp.bfloat16
HI = lax.Precision.HIGHEST

D_MODEL = 1024
D_FF = 2816
GRID_W = 64
GROUP_W = 256
HEAD_DIM = 64
N_HEADS = 4
CHUNK = 64
EPS = 1e-6
N_MOD = 9
GLA_RANK = 16
GLA_NORMALISER = 16.0
S5_CH = 16
S5_GROUPS = 16
S5_STATE = 64
S5_T = 16
S5_PAIRS = S5_GROUPS // 2
SMALL_W = 128
PROJ_W = 3072 + SMALL_W
MOD_ROWS = 8
VMEM_LIMIT = 56 * 1024 * 1024


def _params(sem, vmem=VMEM_LIMIT):
    return pltpu.CompilerParams(dimension_semantics=sem, vmem_limit_bytes=vmem)


def _mm(a, b):
    return jnp.dot(a.astype(BF16), b.astype(BF16), preferred_element_type=F32)


def _mm_nt(a, b):
    return lax.dot_general(a.astype(BF16), b.astype(BF16), (((1,), (1,)), ((), ())), preferred_element_type=F32)


def _mm_tn(a, b):
    return lax.dot_general(a.astype(BF16), b.astype(BF16), (((0,), (0,)), ((), ())), preferred_element_type=F32)


def _mm_hi(a, b):
    return jnp.dot(a, b, precision=HI, preferred_element_type=F32)


def _mm_tn_hi(a, b):
    return lax.dot_general(a, b, (((0,), (0,)), ((), ())), precision=HI, preferred_element_type=F32)


def _rms(t):
    return t * lax.rsqrt(jnp.mean(t * t, axis=-1, keepdims=True) + EPS)


def _modnorm(xv, g, shift, scale):
    return (_rms(xv) * g) * (1.0 + scale) + shift


def _const(shape):
    n = len(shape)
    return pl.BlockSpec(shape, lambda *_: (0,) * n, pipeline_mode=pl.Buffered(1))


def _mod_kernel(c_ref, w_ref, b_ref, o_ref):
    o_ref[...] = _mm_hi(jax.nn.silu(c_ref[...]), w_ref[...]) + b_ref[...]


def _modulation(cc, w_mod, b_mod):
    depth, d, n = w_mod.shape
    tn = 1024
    return pl.pallas_call(
        _mod_kernel, grid=(depth, n // tn),
        in_specs=[pl.BlockSpec((MOD_ROWS, d), lambda l, j: (0, 0)),
                  pl.BlockSpec((None, d, tn), lambda l, j: (l, 0, j)),
                  pl.BlockSpec((None, 1, tn), lambda l, j: (l, 0, j))],
        out_specs=pl.BlockSpec((None, MOD_ROWS, tn), lambda l, j: (l, 0, j)),
        out_shape=jax.ShapeDtypeStruct((depth, MOD_ROWS, n), F32),
        compiler_params=_params(("arbitrary", "arbitrary")), name="modulation",
    )(cc, w_mod, b_mod.reshape(depth, 1, n))


def _mod_spec(k, per_batch):
    if per_batch:
        return pl.BlockSpec((None, None, 1, D_MODEL), lambda b, i: (b, k, 0, 0))
    return pl.BlockSpec((None, None, 1, D_MODEL), lambda b, i: (0, k, 0, 0))


def _ffn_kernel(x_ref, sh_ref, sc_ref, gt_ref, g_ref, wi_ref, wo_ref, *rest, final):
    o_ref = rest[-1]
    xv = x_ref[...]
    h = _modnorm(xv, g_ref[...], sh_ref[...], sc_ref[...]).astype(BF16)
    gu = jnp.dot(h, wi_ref[...], preferred_element_type=F32)
    a = (jax.nn.silu(gu[:, :D_FF]) * gu[:, D_FF:]).astype(BF16)
    y = jnp.dot(a, wo_ref[...], preferred_element_type=F32)
    out = xv + (0.5 * gt_ref[...]) * y
    if final:
        out = _rms(out) * rest[0][...]
    o_ref[...] = out


def _ffn(x, modv, j, g, w_in, w_out, lj, g_final=None):
    bsz, length, d = x.shape
    tm = min(512, length)
    per_batch = modv.shape[0] > 1
    final = g_final is not None
    ins = [x, modv, modv, modv, g, w_in, w_out]
    specs = [pl.BlockSpec((None, tm, d), lambda b, i: (b, i, 0)),
             _mod_spec(3 * j, per_batch), _mod_spec(3 * j + 1, per_batch), _mod_spec(3 * j + 2, per_batch),
             _const((1, d)),
             pl.BlockSpec((None, None, d, 2 * D_FF), lambda b, i: lj + (0, 0), pipeline_mode=pl.Buffered(1)),
             pl.BlockSpec((None, None, D_FF, d), lambda b, i: lj + (0, 0), pipeline_mode=pl.Buffered(1))]
    if final:
        ins.append(g_final)
        specs.append(_const((1, d)))
    return pl.pallas_call(
        functools.partial(_ffn_kernel, final=final), grid=(bsz, length // tm),
        in_specs=specs, out_specs=pl.BlockSpec((None, tm, d), lambda b, i: (b, i, 0)),
        out_shape=jax.ShapeDtypeStruct(x.shape, F32),
        compiler_params=_params(("arbitrary", "arbitrary")), name="ffn",
    )(*ins)


_PROJ_SPLIT = (("gla", 0, 1024, False), ("ml_qk", 1792, 2304, False), ("ml_vo", 2304, 2816, False),
               ("small", 3072, 3200, False), ("ssd_xbc", 1024, 1536, True), ("ssd_z", 1536, 1792, True),
               ("s5", 2816, 3072, False), ("small_cm", 3072, 3200, True))
_ML_QK = (1792, 2304)
S5_HALF = S5_T // 2
S5_HALF_W = S5_PAIRS * S5_HALF * 2 * S5_CH


def _to_s5_chunks(ys):
    yt = ys.T
    pw = 2 * S5_CH
    outs = []
    for pp in range(S5_PAIRS):
        rows = yt[pp * pw:(pp + 1) * pw]
        z = jnp.concatenate([rows[:, s2 * 128:(s2 + 1) * 128] for s2 in range(S5_HALF // 2)], axis=0)
        zt = z.T
        outs.append(jnp.concatenate([zt[0:GRID_W], zt[GRID_W:2 * GRID_W]], axis=1))
    return jnp.concatenate(outs, axis=1)


def _from_s5_chunks(yc):
    pw = 2 * S5_CH
    cw = S5_HALF * pw
    rows = []
    for pp in range(S5_PAIRS):
        blk = yc[:, pp * cw:(pp + 1) * cw]
        z = jnp.concatenate([blk[:, 0:cw // 2], blk[:, cw // 2:cw]], axis=0).T
        rows.append(jnp.concatenate([z[s2 * pw:(s2 + 1) * pw] for s2 in range(S5_HALF // 2)], axis=1))
    return jnp.concatenate(rows, axis=0).T


def _conv3_silu(v, prev, nxt, w_ref, b_ref):
    n = v.shape[0]
    rid = lax.broadcasted_iota(jnp.int32, v.shape, 0)
    vm = jnp.where(rid == 0, prev, pltpu.roll(v, 1, 0))
    vp = jnp.where(rid == n - 1, nxt, pltpu.roll(v, n - 1, 0))
    return jax.nn.silu(w_ref[0:1, :] * vm + w_ref[1:2, :] * v + w_ref[2:3, :] * vp + b_ref[...])


def _inproj_kernel(x_ref, xp_ref, xn_ref, sh_ref, sc_ref, g_ref, w_ref, cw_ref, cb_ref, *outs, colmajor, rt):
    i = pl.program_id(1)
    h = _modnorm(x_ref[...], g_ref[...], sh_ref[...], sc_ref[...]).astype(BF16)
    y = jnp.dot(h, w_ref[...], preferred_element_type=F32)
    halo = jnp.concatenate([xp_ref[...], xn_ref[...]], axis=0)
    hh = _modnorm(halo, g_ref[...], sh_ref[...], sc_ref[...]).astype(BF16)
    lo_qk, hi_qk = _ML_QK
    yh = jnp.dot(hh, w_ref[:, lo_qk:hi_qk], preferred_element_type=F32)
    prev = jnp.where(i > 0, yh[7:8, :], 0.0)
    nxt = jnp.where(i < pl.num_programs(1) - 1, yh[8:9, :], 0.0)
    for o_ref, (name, lo, hi, cm) in zip(outs, _PROJ_SPLIT):
        v = y[:, lo:hi]
        if name == "ml_qk":
            v = _conv3_silu(v, prev, nxt, cw_ref, cb_ref)
        if cm and colmajor:
            v = jnp.swapaxes(v.reshape(rt, GRID_W, hi - lo), 0, 1)
        o_ref[...] = v
        if name == "s5" and colmajor:
            outs[-1][...] = _to_s5_chunks(v)


def _inproj(x, modv, g, w, conv_w, conv_b, colmajor):
    bsz, length, d = x.shape
    tm = min(512, length)
    t8, last8 = tm // 8, length // 8 - 1
    rt = tm // GRID_W
    rows = length // GRID_W
    per_batch = modv.shape[0] > 1
    lo_qk, hi_qk = _ML_QK
    shapes, specs = [], []
    for _, lo, hi, cm in _PROJ_SPLIT:
        n = hi - lo
        if cm and colmajor:
            shapes.append(jax.ShapeDtypeStruct((bsz, GRID_W, rows, n), F32))
            specs.append(pl.BlockSpec((None, GRID_W, rt, n), lambda b, i: (b, 0, i, 0)))
        else:
            shapes.append(jax.ShapeDtypeStruct((bsz, length, n), F32))
            specs.append(pl.BlockSpec((None, tm, n), lambda b, i: (b, i, 0)))
    if colmajor:
        assert rt == S5_HALF and rows % S5_T == 0
        shapes.append(jax.ShapeDtypeStruct((bsz, rows // S5_T, GRID_W, 2 * S5_HALF_W), F32))
        specs.append(pl.BlockSpec((None, None, GRID_W, S5_HALF_W), lambda b, i: (b, i // 2, 0, i % 2)))
    outs = pl.pallas_call(
        functools.partial(_inproj_kernel, colmajor=colmajor, rt=rt), grid=(bsz, length // tm),
        in_specs=[pl.BlockSpec((None, tm, d), lambda b, i: (b, i, 0)),
                  pl.BlockSpec((None, 8, d), lambda b, i: (b, jnp.maximum(i * t8 - 1, 0), 0)),
                  pl.BlockSpec((None, 8, d), lambda b, i: (b, jnp.minimum((i + 1) * t8, last8), 0)),
                  _mod_spec(3, per_batch), _mod_spec(4, per_batch), _const((1, d)), _const((d, PROJ_W)),
                  _const((3, hi_qk - lo_qk)), _const((1, hi_qk - lo_qk))],
        out_specs=specs, out_shape=shapes,
        compiler_params=_params(("arbitrary", "arbitrary")), name="inproj",
    )(x, x, x, modv, modv, g, w, conv_w, conv_b)
    res = {}
    for o, (name, lo, hi, cm) in zip(outs, _PROJ_SPLIT):
        res[name] = o.reshape(bsz, length, hi - lo)
    if colmajor:
        res["s5_chunks"] = outs[-1]
    return res


def _conv_kernel(x_ref, p_ref, n_ref, w_ref, b_ref, o_ref, *, tb):
    i = pl.program_id(1)
    nb = pl.num_programs(1)
    prev = jnp.where(i > 0, p_ref[7:8, :], 0.0)
    nxt = jnp.where(i < nb - 1, n_ref[0:1, :], 0.0)
    o_ref[...] = _conv3_silu(x_ref[...], prev, nxt, w_ref, b_ref)


def _conv_silu(x, w, b):
    bsz, length, ch = x.shape
    tb = min(1024, length)
    t8 = tb // 8
    last8 = length // 8 - 1
    return pl.pallas_call(
        functools.partial(_conv_kernel, tb=tb), grid=(bsz, length // tb),
        in_specs=[pl.BlockSpec((None, tb, ch), lambda bb, i: (bb, i, 0)),
                  pl.BlockSpec((None, 8, ch), lambda bb, i: (bb, jnp.maximum(i * t8 - 1, 0), 0)),
                  pl.BlockSpec((None, 8, ch), lambda bb, i: (bb, jnp.minimum((i + 1) * t8, last8), 0)),
                  _const((3, ch)), _const((1, ch))],
        out_specs=pl.BlockSpec((None, tb, ch), lambda bb, i: (bb, i, 0)),
        out_shape=jax.ShapeDtypeStruct(x.shape, F32),
        compiler_params=_params(("arbitrary", "arbitrary")), name="conv_silu",
    )(x, x, x, w, b)


HW = N_HEADS * HEAD_DIM
MAX_CPB = 8


def _scan_tiling(length):
    cpb = min(MAX_CPB, length // CHUNK)
    assert length % (cpb * CHUNK) == 0
    return cpb, cpb * CHUNK, length // (cpb * CHUNK)


def _scan_block(d, i, nb):
    return jnp.where(d == 0, i, nb - 1 - i)


def _chunk_rows(d, cc, cpb):
    c = jnp.where(d == 0, cc, cpb - 1 - cc)
    return pl.ds(pl.multiple_of(c * CHUNK, CHUNK), CHUNK)


def _scan_consts(d):
    t = lax.broadcasted_iota(jnp.int32, (CHUNK, HW), 0)
    lane = lax.broadcasted_iota(jnp.int32, (CHUNK, HW), 1)
    s = jnp.bitwise_and(lane, HEAD_DIM - 1)
    vis = jnp.where(d == 0, t - s, s - t)
    r64 = lax.broadcasted_iota(jnp.int32, (CHUNK, CHUNK), 0)
    c64 = lax.broadcasted_iota(jnp.int32, (CHUNK, CHUNK), 1)
    rb = lax.broadcasted_iota(jnp.int32, (HW, HW), 0) // HEAD_DIM
    cb = lax.broadcasted_iota(jnp.int32, (HW, HW), 1) // HEAD_DIM
    return dict(
        mask4=vis >= 0,
        maskt4=(vis <= 0).astype(F32),
        eye4=(vis == 0).astype(F32),
        cumsum=(jnp.where(d == 0, r64 - c64, c64 - r64) >= 0).astype(BF16),
        bd=(rb == cb).astype(BF16),
        head=lane // HEAD_DIM)


def _split(x, terms):
    parts = []
    for _ in range(terms - 1):
        parts.append(x.astype(BF16))
        x = x - parts[-1].astype(F32)
    return parts + [x.astype(BF16)]


def _mm_r01(x, w01, terms=2):
    return sum(jnp.dot(p, w01, preferred_element_type=F32) for p in _split(x, terms))


def _mm_l01(w01, x, terms=3):
    return sum(jnp.dot(w01, p, preferred_element_type=F32) for p in _split(x, terms))


def _bd(x, bd01):
    return jnp.concatenate([x.astype(BF16)] * N_HEADS, axis=0) * bd01


def _scan_specs(nb, bsz, tb):
    def blk(width, colblock):
        return pl.BlockSpec((bsz, tb, width), lambda d, i: (0, _scan_block(d, i, nb), colblock))
    out = pl.BlockSpec((None, bsz, tb, GROUP_W), lambda d, i: (d, 0, _scan_block(d, i, nb), 0))
    state = pl.BlockSpec((bsz, None, HW, HW), lambda d, i: (0, d, 0, 0))
    return blk, out, state


def _dir_spec(shape):
    n = len(shape)
    return pl.BlockSpec((None,) + shape, lambda d, i: (d,) + (0,) * n)


def _gla_kernel(q_ref, k_ref, v_ref, sm_ref, w_ref, b_ref, s0_ref, o_ref, sf_ref, s_ref, *, bsz, cpb):
    d = pl.program_id(0)
    i = pl.program_id(1)

    @pl.when(i == 0)
    def _():
        s_ref[...] = s0_ref[...]

    cs = _scan_consts(d)
    bs = range(bsz)
    for cc in range(cpb):
        rows = _chunk_rows(d, cc, cpb)
        sm = sm_ref[:, rows, :].reshape(bsz * CHUNK, SMALL_W)
        lr = jnp.where(d == 0, sm[:, 0:GLA_RANK], sm[:, GLA_RANK:2 * GLA_RANK])
        g = jax.nn.log_sigmoid(_mm(lr, w_ref[...]) + b_ref[...]) * (1.0 / GLA_NORMALISER)
        g = [g[b * CHUNK:(b + 1) * CHUNK] for b in bs]
        bc = _mm_l01(cs["cumsum"], jnp.concatenate(g, axis=1))
        bc = [bc[:, b * HW:(b + 1) * HW] for b in bs]
        bt = [jnp.sum(g[b], axis=0, keepdims=True) for b in bs]
        k = [k_ref[b, rows, :] for b in bs]
        v = [v_ref[b, rows, :].astype(BF16) for b in bs]
        qt = [(q_ref[b, rows, :] * (HEAD_DIM ** -0.5) * jnp.exp(bc[b])).astype(BF16) for b in bs]
        st = [s_ref[b] for b in bs]
        o_st = [_mm_nt(qt[b], st[b].astype(BF16) * cs["bd"]) for b in bs]
        kbd = [_bd(k[b] * jnp.exp(-bc[b]), cs["bd"]) for b in bs]
        att = [jnp.where(cs["mask4"], _mm_nt(qt[b], kbd[b]), 0.0) for b in bs]
        ks = [(k[b] * jnp.exp(bt[b] - bc[b])).astype(BF16) for b in bs]
        upd = [_mm_tn(v[b], ks[b]) for b in bs]
        vbd = [_bd(v[b], cs["bd"]) for b in bs]
        for b in bs:
            o_ref[b, rows, :] = _mm(att[b], vbd[b]) + o_st[b]
        for b in bs:
            s_ref[b] = st[b] * jnp.exp(bt[b]) + upd[b]

    @pl.when(i == pl.num_programs(1) - 1)
    def _():
        sf_ref[...] = s_ref[...]


def _gla(proj, small, w_lr2, b_lr2, s0):
    bsz, length, _ = proj.shape
    cpb, tb, nb = _scan_tiling(length)
    blk, out, state = _scan_specs(nb, bsz, tb)
    return pl.pallas_call(
        functools.partial(_gla_kernel, bsz=bsz, cpb=cpb), grid=(2, nb),
        in_specs=[blk(GROUP_W, 0), blk(GROUP_W, 1), blk(GROUP_W, 2), blk(SMALL_W, 0),
                  _dir_spec((GLA_RANK, GROUP_W)), _dir_spec((1, GROUP_W)), state],
        out_specs=[out, state],
        out_shape=[jax.ShapeDtypeStruct((2, bsz, length, GROUP_W), F32), jax.ShapeDtypeStruct(s0.shape, F32)],
        scratch_shapes=[pltpu.VMEM((bsz, HW, HW), F32)],
        compiler_params=_params(("arbitrary",) * 2), name="gla",
    )(proj, proj, proj, small, w_lr2, b_lr2, s0)


def _ssd_kernel(x_ref, b_ref, c_ref, sm_ref, al_ref, db_ref, s0_ref, o_ref, sf_ref, s_ref, *, bsz, cpb):
    d = pl.program_id(0)
    i = pl.program_id(1)

    @pl.when(i == 0)
    def _():
        s_ref[...] = s0_ref[...]

    cs = _scan_consts(d)
    r = lax.broadcasted_iota(jnp.int32, (SMALL_W, HW), 0)
    c = lax.broadcasted_iota(jnp.int32, (SMALL_W, HW), 1)
    e_dt = (r == 32 + N_HEADS * d + c // HEAD_DIM).astype(BF16)
    e_grp = ((r // HEAD_DIM == c // (2 * HEAD_DIM)) & (r % HEAD_DIM == c % HEAD_DIM)).astype(BF16)
    rg = lax.broadcasted_iota(jnp.int32, (HW, 2 * HEAD_DIM), 0) // (2 * HEAD_DIM)
    cg = lax.broadcasted_iota(jnp.int32, (HW, 2 * HEAD_DIM), 1) // HEAD_DIM
    grp = (rg == cg).astype(BF16)
    nega = -jnp.exp(al_ref[...])
    bs = range(bsz)
    for cc in range(cpb):
        rows = _chunk_rows(d, cc, cpb)
        sm = sm_ref[:, rows, :].reshape(bsz * CHUNK, SMALL_W)
        dt = jax.nn.softplus(_mm_r01(sm, e_dt) + db_ref[...])
        a = dt * nega
        dt = [dt[b * CHUNK:(b + 1) * CHUNK] for b in bs]
        a = [a[b * CHUNK:(b + 1) * CHUNK] for b in bs]
        cum = _mm_l01(cs["cumsum"], jnp.concatenate(a, axis=1))
        cum = [cum[:, b * HW:(b + 1) * HW] for b in bs]
        bm = b_ref[:, rows, :].reshape(bsz * CHUNK, 2 * HEAD_DIM).astype(BF16)
        cm = c_ref[:, rows, :].reshape(bsz * CHUNK, 2 * HEAD_DIM).astype(BF16)
        cmx = _mm(cm, e_grp)
        bmx = _mm(bm, e_grp)
        bm = [bm[b * CHUNK:(b + 1) * CHUNK] for b in bs]
        cm = [cm[b * CHUNK:(b + 1) * CHUNK] for b in bs]
        st = [s_ref[b] for b in bs]
        y_st = [_mm(cmx[b * CHUNK:(b + 1) * CHUNK], st[b].astype(BF16) * cs["bd"]) for b in bs]
        cb4 = [_mm_nt(cm[b], jnp.concatenate([bm[b]] * N_HEADS, axis=0) * grp) for b in bs]
        tot = [jnp.sum(a[b], axis=0, keepdims=True) for b in bs]
        cumr = [jnp.sum(a[b] * cs["maskt4"], axis=0, keepdims=True) for b in bs]
        xdt = [(x_ref[b, rows, :] * dt[b]).astype(BF16) for b in bs]
        bw = [(bmx[b * CHUNK:(b + 1) * CHUNK] * jnp.exp(tot[b] - cum[b])).astype(BF16) for b in bs]
        upd = [_mm_tn(bw[b], xdt[b]) for b in bs]
        xbd = [_bd(xdt[b], cs["bd"]) for b in bs]
        sc = [cb4[b] * jnp.exp(jnp.where(cs["mask4"], cum[b] - cumr[b], -jnp.inf)) for b in bs]
        for b in bs:
            o_ref[b, rows, :] = _mm(sc[b], xbd[b]) + jnp.exp(cum[b]) * y_st[b]
        for b in bs:
            s_ref[b] = st[b] * jnp.exp(tot[b]) + upd[b]

    @pl.when(i == pl.num_programs(1) - 1)
    def _():
        sf_ref[...] = s_ref[...]


def _per_head_lanes(t):
    return jnp.repeat(t.astype(F32), HEAD_DIM, axis=-1)[:, None, :]


def _ssd(xbc, small, a_log, dt_bias, s0):
    bsz, length, _ = xbc.shape
    cpb, tb, nb = _scan_tiling(length)
    blk, out, state = _scan_specs(nb, bsz, tb)
    return pl.pallas_call(
        functools.partial(_ssd_kernel, bsz=bsz, cpb=cpb), grid=(2, nb),
        in_specs=[blk(GROUP_W, 0), blk(128, 2), blk(128, 3), blk(SMALL_W, 0),
                  _dir_spec((1, HW)), _dir_spec((1, HW)), state],
        out_specs=[out, state],
        out_shape=[jax.ShapeDtypeStruct((2, bsz, length, GROUP_W), F32), jax.ShapeDtypeStruct(s0.shape, F32)],
        scratch_shapes=[pltpu.VMEM((bsz, HW, HW), F32)],
        compiler_params=_params(("arbitrary",) * 2), name="ssd",
    )(xbc, xbc, xbc, small, _per_head_lanes(a_log), _per_head_lanes(dt_bias), s0)


def _mlstm_kernel(q_ref, k_ref, v_ref, sm_ref, gi_ref, gf_ref, c0_ref, nm0_ref, o_ref, cf_ref, nmf_ref,
                  c_ref, nm_ref, *, bsz, cpb):
    d = pl.program_id(0)
    i = pl.program_id(1)

    @pl.when(i == 0)
    def _():
        c_ref[...] = c0_ref[...]
        nm_ref[...] = nm0_ref[...]

    cs = _scan_consts(d)
    r = lax.broadcasted_iota(jnp.int32, (SMALL_W, 2 * HW), 0)
    c = lax.broadcasted_iota(jnp.int32, (SMALL_W, 2 * HW), 1)
    gate = 40 + 2 * N_HEADS * d + N_HEADS * (c // HW) + (c % HW) // HEAD_DIM
    e_if = (r == gate).astype(BF16)
    rid = lax.broadcasted_iota(jnp.int32, (CHUNK, 1), 0)
    last = rid == jnp.where(d == 0, CHUNK - 1, 0)
    bs = range(bsz)

    def per_b(t):
        return [t[b * CHUNK:(b + 1) * CHUNK] for b in bs]

    for cc in range(cpb):
        rows = _chunk_rows(d, cc, cpb)
        sm = sm_ref[:, rows, :].reshape(bsz * CHUNK, SMALL_W)
        gates = _mm_r01(sm, e_if)
        gi = per_b(gates[:, :HW] + gi_ref[...])
        lf = per_b(jax.nn.log_sigmoid(gates[:, HW:] + gf_ref[...]))
        fc = _mm_l01(cs["cumsum"], jnp.concatenate(lf, axis=1))
        fc = [fc[:, b * HW:(b + 1) * HW] for b in bs]
        q = [q_ref[b, rows, :] for b in bs]
        qb = [q[b].astype(BF16) for b in bs]
        k = [k_ref[b, rows, :] * (HEAD_DIM ** -0.5) for b in bs]
        v = [v_ref[b, rows, :].astype(BF16) for b in bs]
        cst = [c_ref[b] for b in bs]
        ns = [nm_ref[b, 0:1, :] for b in bs]
        ms = [nm_ref[b, 1:2, :] for b in bs]
        n_st = [_mm(qb[b], cst[b].astype(BF16) * cs["bd"]) for b in bs]
        qn = [_split(q[b] * ns[b], 2) for b in bs]
        qk = [_mm_nt(qb[b], _bd(k[b], cs["bd"])) for b in bs]
        ftot = [jnp.sum(lf[b], axis=0, keepdims=True) for b in bs]
        fr = [jnp.sum(lf[b] * cs["maskt4"], axis=0, keepdims=True) for b in bs]
        ir = [jnp.sum(gi[b] * cs["eye4"], axis=0, keepdims=True) for b in bs]
        li = [jnp.where(cs["mask4"], fc[b] - fr[b] + ir[b], -jnp.inf) for b in bs]
        m = []
        for b in bs:
            rmax = jnp.zeros_like(li[b])
            for h in range(N_HEADS):
                mh = jnp.max(li[b][:, h * HEAD_DIM:(h + 1) * HEAD_DIM], axis=-1, keepdims=True)
                rmax = jnp.where(cs["head"] == h, mh, rmax)
            m.append(jnp.maximum(fc[b] + ms[b], rmax))
        m_new = [jnp.sum(jnp.where(last, m[b], 0.0), axis=0, keepdims=True) for b in bs]
        kw = [(k[b] * jnp.exp(ftot[b] - fc[b] + gi[b] - m_new[b])) for b in bs]
        upd = [_mm_tn(kw[b], v[b]) for b in bs]
        wf = [qk[b] * jnp.exp(li[b] - m[b]) for b in bs]
        w = [wf[b].astype(BF16) for b in bs]
        w_lo = [(wf[b] - w[b].astype(F32)).astype(BF16) for b in bs]
        vbd = [_bd(v[b], cs["bd"]) for b in bs]
        n_in = [_mm(w[b], vbd[b]) for b in bs]
        sums = [jnp.dot(jnp.concatenate(qn[b] + [w[b], w_lo[b]], axis=0), cs["bd"], preferred_element_type=F32)
                for b in bs]
        for b in bs:
            winter = jnp.exp(fc[b] + ms[b] - m[b])
            d_st = sums[b][0:CHUNK] + sums[b][CHUNK:2 * CHUNK]
            d_in = sums[b][2 * CHUNK:3 * CHUNK] + sums[b][3 * CHUNK:]
            den = d_in + winter * d_st
            o_ref[b, rows, :] = (n_in[b] + winter * n_st[b]) / jnp.maximum(jnp.abs(den), jnp.exp(-m[b]))
        for b in bs:
            decay = jnp.exp(ftot[b] + ms[b] - m_new[b])
            c_ref[b] = cst[b] * decay + upd[b]
            nm_ref[b, 0:1, :] = decay * ns[b] + jnp.sum(kw[b], axis=0, keepdims=True)
            nm_ref[b, 1:2, :] = m_new[b]

    @pl.when(i == pl.num_programs(1) - 1)
    def _():
        cf_ref[...] = c_ref[...]
        nmf_ref[...] = nm_ref[...]


def _mlstm(qk, vo, small, gate_bias, c0, nm0):
    bsz, length, _ = qk.shape
    cpb, tb, nb = _scan_tiling(length)
    blk, out, state = _scan_specs(nb, bsz, tb)
    nm_spec = pl.BlockSpec((bsz, None, 8, HW), lambda d, i: (0, d, 0, 0))
    return pl.pallas_call(
        functools.partial(_mlstm_kernel, bsz=bsz, cpb=cpb), grid=(2, nb),
        in_specs=[blk(GROUP_W, 0), blk(GROUP_W, 1), blk(GROUP_W, 0), blk(SMALL_W, 0),
                  _dir_spec((1, HW)), _dir_spec((1, HW)), state, nm_spec],
        out_specs=[out, state, nm_spec],
        out_shape=[jax.ShapeDtypeStruct((2, bsz, length, GROUP_W), F32), jax.ShapeDtypeStruct(c0.shape, F32),
                   jax.ShapeDtypeStruct(nm0.shape, F32)],
        scratch_shapes=[pltpu.VMEM((bsz, HW, HW), F32), pltpu.VMEM((bsz, 8, HW), F32)],
        compiler_params=_params(("arbitrary",) * 2), name="mlstm",
    )(qk, qk, vo, small, _per_head_lanes(gate_bias[:, 0]), _per_head_lanes(gate_bias[:, 1]), c0, nm0)


def _s5_slot(s, gl):
    return (s // S5_HALF) * (S5_HALF * 2 * S5_CH) + (s % 2) * 128 + ((s % S5_HALF) // 2) * 2 * S5_CH + gl * S5_CH


def _s5_prep_kernel(arc, aic, arr, air, lst, brn, bin_, bri, bii, ctr, cti, m_ref, q_ref, p_ref, a16r_ref, a16i_ref):
    n, t16 = S5_STATE, S5_T
    gw = S5_T * S5_CH
    lane_t = lax.broadcasted_iota(jnp.int32, (n, gw), 1) // S5_CH
    sel = (lax.broadcasted_iota(jnp.int32, (S5_CH, gw), 1) % S5_CH
           == lax.broadcasted_iota(jnp.int32, (S5_CH, gw), 0)).astype(F32)
    lane_blk = lax.broadcasted_iota(jnp.int32, (S5_CH, gw), 1) // S5_CH
    lane128 = lax.broadcasted_iota(jnp.int32, (S5_CH, 2 * n), 1)
    m_ref[...] = jnp.zeros_like(m_ref)
    q_ref[...] = jnp.zeros_like(q_ref)
    p_ref[...] = jnp.zeros_like(p_ref)

    def discretise(a_re, a_im, step):
        a_re = jnp.minimum(a_re, -1e-4)
        mag = jnp.exp(a_re * step)
        ab_re, ab_im = mag * jnp.cos(a_im * step), mag * jnp.sin(a_im * step)
        den = a_re * a_re + a_im * a_im
        nr, ni = ab_re - 1.0, ab_im
        return ab_re, ab_im, (nr * a_re + ni * a_im) / den, (ni * a_re - nr * a_im) / den

    def powers(ab_re, ab_im):
        pr, pi = [jnp.ones_like(ab_re)], [jnp.zeros_like(ab_re)]
        for _ in range(t16):
            pr.append(pr[-1] * ab_re - pi[-1] * ab_im)
            pi.append(pr[-2] * ab_im + pi[-1] * ab_re)
        return pr, pi

    for d in range(2):
        step_r = jnp.exp(jnp.concatenate([jnp.broadcast_to(lst[d, gl], (1, n)) for gl in range(2)], axis=1))
        ab_re, ab_im, cf_re, cf_im = discretise(arr[d], air[d], step_r)
        prr, pir = powers(ab_re, ab_im)
        a16r_ref[d] = prr[t16]
        a16i_ref[d] = pir[t16]
        bt_re = cf_re * bri[...] - cf_im * bii[...]
        bt_im = cf_re * bii[...] + cf_im * bri[...]
        for s in range(t16):
            e = t16 - 1 - s if d == 0 else s
            qre = bt_re * prr[e] - bt_im * pir[e]
            qim = bt_re * pir[e] + bt_im * prr[e]
            for gl in range(2):
                keep = (lane128 // n) == gl
                r0 = _s5_slot(s, gl)
                q_ref[d, r0:r0 + S5_CH, 0:2 * n] = jnp.where(keep, qre, 0.0).astype(q_ref.dtype)
                q_ref[d, r0:r0 + S5_CH, 2 * n:4 * n] = jnp.where(keep, qim, 0.0).astype(q_ref.dtype)
        for gl in range(2):
            ab_re, ab_im, cf_re, cf_im = discretise(arc[d, gl], aic[d, gl], jnp.exp(lst[d, gl]))
            prc, pic = powers(ab_re, ab_im)
            bb_re = cf_re * brn[gl] - cf_im * bin_[gl]
            bb_im = cf_re * bin_[gl] + cf_im * brn[gl]
            ct_re = _mm_hi(ctr[gl], sel)
            ct_im = _mm_hi(cti[gl], sel)

            def response(exps):
                p_re = jnp.zeros((n, gw), F32)
                p_im = jnp.zeros((n, gw), F32)
                for t in range(t16):
                    p_re = jnp.where(lane_t == t, prc[exps[t]], p_re)
                    p_im = jnp.where(lane_t == t, pic[exps[t]], p_im)
                return ct_re * p_re - ct_im * p_im, ct_re * p_im + ct_im * p_re

            e_re, e_im = response([t if d == 0 else t16 - 1 - t for t in range(t16)])
            r0v = _mm_tn_hi(bb_re, e_re) - _mm_tn_hi(bb_im, e_im)
            for s in range(t16):
                if d == 0:
                    blk = jnp.where(lane_blk >= s, pltpu.roll(r0v, (S5_CH * s) % gw, 1), 0.0)
                else:
                    blk = jnp.where(lane_blk <= s, pltpu.roll(r0v, (gw - S5_CH * (t16 - 1 - s)) % gw, 1), 0.0)
                r0 = _s5_slot(s, gl)
                m_ref[d, r0:r0 + S5_CH, gl * gw:(gl + 1) * gw] = blk.astype(m_ref.dtype)
            c_re, c_im = response([t + 1 if d == 0 else t16 - t for t in range(t16)])
            p_ref[d, gl * n:(gl + 1) * n, gl * gw:(gl + 1) * gw] = c_re.astype(p_ref.dtype)
            p_ref[d, 2 * n + gl * n:2 * n + (gl + 1) * n, gl * gw:(gl + 1) * gw] = (-c_im).astype(p_ref.dtype)
    c = lax.broadcasted_iota(jnp.int32, (2 * gw, 2 * gw), 1)
    half = S5_HALF * 2 * S5_CH
    t_of_c = S5_HALF * (c // half) + 2 * ((c % 128) // (2 * S5_CH)) + (c % half) // 128
    src = ((c % (2 * S5_CH)) // S5_CH) * gw + t_of_c * S5_CH + c % S5_CH
    perm = (lax.broadcasted_iota(jnp.int32, (2 * gw, 2 * gw), 0) == src).astype(BF16)
    for d in range(2):
        m_ref[d] = jnp.dot(m_ref[d], perm, preferred_element_type=F32).astype(m_ref.dtype)
        p_ref[d] = jnp.dot(p_ref[d], perm, preferred_element_type=F32).astype(p_ref.dtype)


def _s5_prep(a_re, a_im, log_step, b_re, b_im, c_re, c_im):
    n, g, ch, pr = S5_STATE, S5_GROUPS, S5_CH, S5_PAIRS
    f = lambda t: t.astype(F32)
    arc, aic = f(a_re)[..., None], f(a_im)[..., None]
    arr, air = f(a_re).reshape(2, pr, 1, 2 * n), f(a_im).reshape(2, pr, 1, 2 * n)
    lst = f(log_step).reshape(2, g, 1, 1)
    pair_rows = lambda t: f(t).reshape(pr, 2, n, ch).transpose(0, 3, 1, 2).reshape(pr, ch, 2 * n)
    ctr, cti = f(c_re).transpose(0, 2, 1), f(c_im).transpose(0, 2, 1)
    gw2 = 2 * S5_T * S5_CH
    spec_c = pl.BlockSpec((2, 2, n, 1), lambda p: (0, p, 0, 0))
    spec_r = pl.BlockSpec((2, None, 1, 2 * n), lambda p: (0, p, 0, 0))
    spec_g = pl.BlockSpec((2, n, ch), lambda p: (p, 0, 0))
    spec_p = pl.BlockSpec((None, ch, 2 * n), lambda p: (p, 0, 0))
    outs = pl.pallas_call(
        _s5_prep_kernel, grid=(pr,),
        in_specs=[spec_c, spec_c, spec_r, spec_r, pl.BlockSpec((2, 2, 1, 1), lambda p: (0, p, 0, 0)),
                  spec_g, spec_g, spec_p, spec_p, spec_g, spec_g],
        out_specs=[pl.BlockSpec((2, None, gw2, gw2), lambda p: (0, p, 0, 0)),
                   pl.BlockSpec((2, None, gw2, 4 * n), lambda p: (0, p, 0, 0)),
                   pl.BlockSpec((2, None, 4 * n, gw2), lambda p: (0, p, 0, 0)),
                   pl.BlockSpec((2, None, 1, 2 * n), lambda p: (0, p, 0, 0)),
                   pl.BlockSpec((2, None, 1, 2 * n), lambda p: (0, p, 0, 0))],
        out_shape=[jax.ShapeDtypeStruct((2, pr, gw2, gw2), BF16), jax.ShapeDtypeStruct((2, pr, gw2, 4 * n), BF16),
                   jax.ShapeDtypeStruct((2, pr, 4 * n, gw2), BF16),
                   jax.ShapeDtypeStruct((2, pr, 1, 2 * n), F32), jax.ShapeDtypeStruct((2, pr, 1, 2 * n), F32)],
        compiler_params=_params(("arbitrary",)), name="s5_prep",
    )(arc, aic, arr, air, lst, f(b_re), f(b_im), pair_rows(b_re), pair_rows(b_im), ctr, cti)
    m, q, p, a16r, a16i = outs
    return m, q, p, a16r.reshape(2, 1, g * n), a16i.reshape(2, 1, g * n)


def _s5_state_kernel(u0_ref, u1_ref, q_ref, vr_ref, vi_ref):
    half = q_ref.shape[0] // 2
    v = (jnp.dot(u0_ref[...], q_ref[0:half, :], preferred_element_type=F32)
         + jnp.dot(u1_ref[...], q_ref[half:, :], preferred_element_type=F32))
    vr_ref[...] = v[:, 0:128]
    vi_ref[...] = v[:, 128:256]


def _s5_scan_kernel(vr_ref, vi_ref, ar_ref, ai_ref, h0r_ref, h0i_ref, hr_ref, hi_ref, fr_ref, fi_ref, *, nch, bsz):
    d = pl.program_id(0)
    ar = jnp.broadcast_to(ar_ref[...], h0r_ref.shape)
    ai = jnp.broadcast_to(ai_ref[...], h0r_ref.shape)
    cpt = 8 // bsz
    ntile = nch // cpt

    def run(reverse):
        def body(kk, carry):
            hr, hi = carry
            t = ntile - 1 - kk if reverse else kk
            rows = pl.ds(pl.multiple_of(t * 8, 8), 8)
            vr8, vi8 = vr_ref[rows, :], vi_ref[rows, :]
            in_r, in_i = [None] * cpt, [None] * cpt
            for j in (range(cpt - 1, -1, -1) if reverse else range(cpt)):
                in_r[j], in_i[j] = hr, hi
                sl = slice(j * bsz, (j + 1) * bsz)
                hr, hi = ar * hr - ai * hi + vr8[sl], ar * hi + ai * hr + vi8[sl]
            hr_ref[rows, :] = jnp.concatenate(in_r, axis=0)
            hi_ref[rows, :] = jnp.concatenate(in_i, axis=0)
            return hr, hi

        hr, hi = lax.fori_loop(0, ntile, body, (h0r_ref[...], h0i_ref[...]))
        fr_ref[...] = hr
        fi_ref[...] = hi

    @pl.when(d == 0)
    def _():
        run(False)

    @pl.when(d == 1)
    def _():
        run(True)


def _s5_out_kernel(u0_ref, u1_ref, m_ref, p_ref, hr_ref, hi_ref, y_ref):
    u = jnp.concatenate([u0_ref[...], u1_ref[...]], axis=-1)
    y = None
    for d in range(2):
        hcat = jnp.concatenate([hr_ref[d], hi_ref[d]], axis=-1).astype(BF16)
        t = jnp.dot(u, m_ref[d], preferred_element_type=F32) + jnp.dot(hcat, p_ref[d], preferred_element_type=F32)
        y = t if y is None else y + t
    y_ref[...] = y


def _s5(ug, mats, h0r, h0i, bsz):
    m, q, p, a16r, a16i = mats
    rows = ug.shape[0]
    nch = rows // bsz
    gn = S5_GROUPS * S5_STATE
    gw2 = 2 * S5_T * S5_CH
    hw = gw2 // 2
    vr, vi = pl.pallas_call(
        _s5_state_kernel, grid=(2, S5_PAIRS),
        in_specs=[pl.BlockSpec((rows, hw), lambda d, pp: (0, pp)),
                  pl.BlockSpec((rows, hw), lambda d, pp: (0, S5_PAIRS + pp)),
                  pl.BlockSpec((None, None, gw2, 256), lambda d, pp: (d, pp, 0, 0))],
        out_specs=[pl.BlockSpec((None, rows, 128), lambda d, pp: (d, 0, pp))] * 2,
        out_shape=[jax.ShapeDtypeStruct((2, rows, gn), F32)] * 2,
        compiler_params=_params(("arbitrary", "arbitrary")), name="s5_state",
    )(ug, ug, q)
    lb = 256
    blk = pl.BlockSpec((None, rows, lb), lambda d, j: (d, 0, j))
    vec = pl.BlockSpec((None, 1, lb), lambda d, j: (d, 0, j))
    st = pl.BlockSpec((None, bsz, lb), lambda d, j: (d, 0, j))
    hr, hi, fr, fi = pl.pallas_call(
        functools.partial(_s5_scan_kernel, nch=nch, bsz=bsz), grid=(2, gn // lb),
        in_specs=[blk, blk, vec, vec, st, st], out_specs=[blk, blk, st, st],
        out_shape=[jax.ShapeDtypeStruct((2, rows, gn), F32)] * 2 + [jax.ShapeDtypeStruct((2, bsz, gn), F32)] * 2,
        compiler_params=_params(("arbitrary", "arbitrary")), name="s5_scan",
    )(vr, vi, a16r, a16i, h0r, h0i)
    y = pl.pallas_call(
        _s5_out_kernel, grid=(S5_PAIRS, 2),
        in_specs=[pl.BlockSpec((rows, hw), lambda pp, hf: (0, pp)),
                  pl.BlockSpec((rows, hw), lambda pp, hf: (0, S5_PAIRS + pp)),
                  pl.BlockSpec((2, None, gw2, hw), lambda pp, hf: (0, pp, 0, hf)),
                  pl.BlockSpec((2, None, 256, hw), lambda pp, hf: (0, pp, 0, hf)),
                  pl.BlockSpec((2, rows, 128), lambda pp, hf: (0, 0, pp)),
                  pl.BlockSpec((2, rows, 128), lambda pp, hf: (0, 0, pp))],
        out_specs=pl.BlockSpec((rows, hw), lambda pp, hf: (0, hf * S5_PAIRS + pp)),
        out_shape=jax.ShapeDtypeStruct((rows, 2 * S5_HALF_W), F32),
        compiler_params=_params(("arbitrary", "arbitrary")), name="s5_out",
    )(ug, ug, m, p, hr, hi)
    return y, fr, fi


def _s5_rows_from_grid(uc):
    bsz, r, w, lanes = uc.shape
    return uc.transpose(2, 1, 0, 3).reshape(w * r * bsz, lanes).astype(BF16)


def _s5_rows_to_grid(y, bsz):
    r = y.shape[0] // (GRID_W * bsz)
    return y.reshape(GRID_W, r, bsz, y.shape[1]).transpose(2, 1, 0, 3)


def _s5_rows_from_tokens(u):
    bsz, length, _ = u.shape
    nch = length // S5_T
    t = u.reshape(bsz, nch, 2, S5_HALF // 2, 2, S5_PAIRS, 2, S5_CH)
    t = t.transpose(1, 0, 2, 5, 4, 3, 6, 7)
    return t.reshape(nch * bsz, 2 * S5_HALF_W).astype(BF16)


def _s5_rows_to_tokens(y, bsz):
    nch = y.shape[0] // bsz
    t = y.reshape(nch, bsz, 2, S5_PAIRS, 2, S5_HALF // 2, 2, S5_CH)
    t = t.transpose(1, 0, 2, 5, 4, 3, 6, 7)
    return t.reshape(bsz, nch * S5_T, S5_GROUPS * S5_CH)


def _post_kernel(x_ref, gt_ref, go_ref, gr_ref, sy_ref, sx_ref, sz_ref, mh_ref, mo_ref, y5_ref, u5_ref,
                 gg_ref, sd_ref, sg_ref, mg_ref, d5_ref, wg_ref, bg_ref, avg_ref, wo_ref, o_ref, *, colmajor, rt):
    def rm(v):
        if not colmajor:
            return v
        return jnp.swapaxes(v, 0, 1).reshape(rt * GRID_W, v.shape[-1])

    def head_rms(t):
        return t * lax.rsqrt(_mm_r01(t * t, avg_ref[...], terms=3) + EPS)

    a = head_rms(go_ref[0] + go_ref[1]) * gg_ref[...] * jax.nn.silu(gr_ref[...])
    ys = rm(sy_ref[0] + sy_ref[1] + sd_ref[...] * sx_ref[...])
    b = _rms(ys * jax.nn.silu(rm(sz_ref[...]))) * sg_ref[...]
    m = head_rms(mh_ref[0] + mh_ref[1]) * mg_ref[...] * jax.nn.sigmoid(mo_ref[...])
    y5 = (_from_s5_chunks(y5_ref[...]) if colmajor else y5_ref[...]) + d5_ref[...] * u5_ref[...]
    gl = jax.nn.gelu(y5)
    dd = gl * jax.nn.sigmoid(_mm(gl, wg_ref[...]) + bg_ref[...])
    cat = jnp.concatenate([a, b, m, dd], axis=-1).astype(BF16)
    o_ref[...] = x_ref[...] + gt_ref[...] * jnp.dot(cat, wo_ref[...], preferred_element_type=F32)


def _post(x, modv, gla_o, gla_proj, ssd_y, ssd_xbc, ssd_z, ml_h, ml_vo, s5_y, s5_u, pw, colmajor):
    bsz, length, d = x.shape
    tm = min(512, length)
    rt = tm // GRID_W
    rows = length // GRID_W
    per_batch = modv.shape[0] > 1
    w = GROUP_W

    def rmspec(colblock, dirs=False):
        if dirs:
            return pl.BlockSpec((2, None, tm, w), lambda b, i: (0, b, i, colblock))
        return pl.BlockSpec((None, tm, w), lambda b, i: (b, i, colblock))

    def cmspec(colblock, dirs=False):
        if not colmajor:
            return rmspec(colblock, dirs)
        if dirs:
            return pl.BlockSpec((2, None, GRID_W, rt, w), lambda b, i: (0, b, 0, i, colblock))
        return pl.BlockSpec((None, GRID_W, rt, w), lambda b, i: (b, 0, i, colblock))

    def cm(t):
        return t.reshape(t.shape[:-2] + (GRID_W, rows, t.shape[-1])) if colmajor else t

    s5_spec = pl.BlockSpec((None, None, GRID_W, S5_HALF_W), lambda b, i: (b, i // 2, 0, i % 2)) if colmajor else rmspec(0)
    return pl.pallas_call(
        functools.partial(_post_kernel, colmajor=colmajor, rt=rt), grid=(bsz, length // tm),
        in_specs=[pl.BlockSpec((None, tm, d), lambda b, i: (b, i, 0)), _mod_spec(5, per_batch),
                  rmspec(0, True), rmspec(3), cmspec(0, True), cmspec(0), cmspec(0),
                  rmspec(0, True), rmspec(1), s5_spec, rmspec(0),
                  _const((1, w)), _const((1, w)), _const((1, w)), _const((1, w)), _const((1, w)),
                  _const((w, w)), _const((1, w)), _const((w, w)), _const((4 * w, d))],
        out_specs=pl.BlockSpec((None, tm, d), lambda b, i: (b, i, 0)),
        out_shape=jax.ShapeDtypeStruct(x.shape, F32),
        compiler_params=_params(("arbitrary", "arbitrary")), name="post",
    )(x, modv, gla_o, gla_proj, cm(ssd_y), cm(ssd_xbc), cm(ssd_z), ml_h, ml_vo, s5_y, s5_u,
      pw["gla_g"], pw["ssd_d"], pw["ssd_g"], pw["ml_g"], pw["s5_d"], pw["w_glu"], pw["b_glu"], pw["avg"],
      pw["w_out"])


def _token_mixing(h_in, modv, g1, lw, states, colmajor):
    bsz, length, _ = h_in.shape
    pr = _inproj(h_in, modv, g1, lw["w_in"], lw["ml_conv_w"], lw["ml_conv_b"], colmajor)
    xbc = _conv_silu(pr["ssd_xbc"], lw["ssd_conv_w"], lw["ssd_conv_b"])
    mqk = pr["ml_qk"]
    gla_o, gla_s = _gla(pr["gla"], pr["small"], lw["gla_w_lr2"], lw["gla_b_lr2"], states["gla"])
    ssd_y, ssd_s = _ssd(xbc, pr["small_cm"], lw["ssd_a_log"], lw["ssd_dt_bias"], states["ssd"])
    ml_h, ml_c, ml_nm = _mlstm(mqk, pr["ml_vo"], pr["small"], lw["ml_gate_bias"], states["ml_c"], states["ml_nm"])
    ug = _s5_rows_from_grid(pr["s5_chunks"]) if colmajor else _s5_rows_from_tokens(pr["s5"])
    y5, s5r, s5i = _s5(ug, lw["s5_mats"], states["s5_r"], states["s5_i"], bsz)
    y5 = _s5_rows_to_grid(y5, bsz) if colmajor else _s5_rows_to_tokens(y5, bsz)
    outs = dict(gla_o=gla_o, gla_proj=pr["gla"], ssd_y=ssd_y, ssd_xbc=xbc, ssd_z=pr["ssd_z"], ml_h=ml_h,
                ml_vo=pr["ml_vo"], s5_y=y5, s5_u=pr["s5"])
    finals = dict(gla=gla_s, ssd=ssd_s, ml_c=ml_c, ml_nm=ml_nm, s5_r=s5r, s5_i=s5i)
    return outs, finals


def _proj_weight(w):
    cols = [w[:, 0:1024], w[:, 1056:1568], w[:, 1568:1824], w[:, 1832:2344], w[:, 2344:2856], w[:, 2872:3128],
            w[:, 1024:1056], w[:, 1824:1832], w[:, 2856:2872], jnp.zeros((w.shape[0], SMALL_W - 56), w.dtype)]
    return jnp.concatenate(cols, axis=1).astype(BF16)


def kernel(x, c, ctx, c_ctx, w_mod, b_mod, g_norm, ffn_w_in, ffn_w_out, w_in, w_out, gla_w_lr2, gla_b_lr2,
           gla_g_norm, ssd_conv_w, ssd_conv_b, ssd_a_log, ssd_dt_bias, ssd_d, ssd_g_norm, ml_conv_w, ml_conv_b,
           ml_gate_bias, ml_g_norm, s5_a_re, s5_a_im, s5_log_step, s5_b_re, s5_b_im, s5_c_re, s5_c_im, s5_d,
           s5_w_glu, s5_b_glu, g_final):
    bsz, length, d = x.shape
    depth = w_mod.shape[0]
    assert bsz + 1 <= MOD_ROWS and d == D_MODEL and length % 1024 == 0 and ctx.shape[1] % CHUNK == 0
    cc = jnp.concatenate([c, c_ctx[None, :], jnp.zeros((MOD_ROWS - bsz - 1, d), F32)], axis=0)
    mod = _modulation(cc, w_mod, b_mod)
    avg = jnp.kron(jnp.eye(N_HEADS, dtype=F32), jnp.full((HEAD_DIM, HEAD_DIM), 1.0 / HEAD_DIM, F32)).astype(BF16)
    row = lambda t: t.reshape(1, -1).astype(F32)
    wi, wo = ffn_w_in.astype(BF16), ffn_w_out.astype(BF16)
    zero_states = dict(
        gla=jnp.zeros((bsz, 2, HW, HW), F32), ssd=jnp.zeros((bsz, 2, HW, HW), F32),
        ml_c=jnp.zeros((bsz, 2, HW, HW), F32), ml_nm=jnp.zeros((bsz, 2, 8, HW), F32),
        s5_r=jnp.zeros((2, bsz, S5_GROUPS * S5_STATE), F32), s5_i=jnp.zeros((2, bsz, S5_GROUPS * S5_STATE), F32))
    for l in range(depth):
        last = l == depth - 1
        mod_x = mod[l, :bsz].reshape(bsz, N_MOD, 1, d)
        mod_c = mod[l, bsz:bsz + 1].reshape(1, N_MOD, 1, d)
        g = [row(g_norm[l, j]) for j in range(3)]
        lw = dict(
            w_in=_proj_weight(w_in[l]), gla_w_lr2=gla_w_lr2[l], gla_b_lr2=gla_b_lr2[l][:, None, :],
            ssd_conv_w=ssd_conv_w[l], ssd_conv_b=row(ssd_conv_b[l]), ssd_a_log=ssd_a_log[l].astype(F32),
            ssd_dt_bias=ssd_dt_bias[l], ml_conv_w=ml_conv_w[l], ml_conv_b=row(ml_conv_b[l]),
            ml_gate_bias=ml_gate_bias[l],
            s5_mats=_s5_prep(s5_a_re[l], s5_a_im[l], s5_log_step[l], s5_b_re[l], s5_b_im[l], s5_c_re[l], s5_c_im[l]))
        pw = dict(gla_g=row(gla_g_norm[l]), ssd_d=row(jnp.repeat(ssd_d[l], HEAD_DIM)), ssd_g=row(ssd_g_norm[l]),
                  ml_g=row(ml_g_norm[l]), s5_d=row(s5_d[l]), w_glu=s5_w_glu[l].astype(BF16), b_glu=row(s5_b_glu[l]),
                  avg=avg, w_out=w_out[l].astype(BF16))
        x = _ffn(x, mod_x, 0, g[0], wi, wo, (l, 0))
        ctx = _ffn(ctx, mod_c, 0, g[0], wi, wo, (l, 0))
        outs_c, finals = _token_mixing(ctx, mod_c, g[1], lw, zero_states, colmajor=False)
        outs_x, _ = _token_mixing(x, mod_x, g[1], lw, finals, colmajor=True)
        x = _post(x, mod_x, pw=pw, colmajor=True, **outs_x)
        x = _ffn(x, mod_x, 2, g[2], wi, wo, (l, 1), g_final=row(g_final) if last else None)
        if not last:
            ctx = _post(ctx, mod_c, pw=pw, colmajor=False, **outs_c)
            ctx = _ffn(ctx, mod_c, 2, g[2], wi, wo, (l, 1))
    return x
```
